```python
import math
import jax
import jax.numpy as jnp
from jax import lax
import numpy as np

D_MODEL = 2048
BATCH = 32
SEQ = 256
DEPTH = 2
DEC_BATCH = 2
DEC_SEQ = 2048
PAST_LEN = 256

GRID_W = 64
HEAD_DIM = 64
ATTN_SCALE = HEAD_DIM ** -0.5
NA_HEADS = 8
NA_WIN_ROWS = 8
NA_WIN_COLS = 16
HY_WIDTH = 1024
HY_ORDER = 2
HY_SHORT = 3
HY_POS_BANDS = 16
HY_POS_DIM = 1 + 2 * HY_POS_BANDS
HY_FFN = 64
HY_FAST_DECAY = 0.3
HY_SLOW_DECAY = 1.5
HY_DECAY_TARGET = 1e-2
HY_MOD_SHIFT = 0.05
SWA_HEADS = 8
SWA_KV_HEADS = 2
SWA_GROUP = SWA_HEADS // SWA_KV_HEADS
SWA_WINDOW = 128
SWA_BLOCK = 128
CTX_Q_BLOCK = 128
NA_W = NA_HEADS * HEAD_DIM
SWA_W = SWA_HEADS * HEAD_DIM
SWA_KV_W = SWA_KV_HEADS * HEAD_DIM
MIX_W = NA_W + HY_WIDTH + SWA_W
IN_W = 3 * NA_W + 3 * HY_WIDTH + SWA_W + 2 * SWA_KV_W
D_FF = 7168
N_EXPERTS = 8
TOP_K = 2
D_FF_EXPERT = 7168
N_DENSE = (DEPTH + 1) // 2
N_MOE = DEPTH // 2
ROPE_THETA = 10000.0
LN_EPS = 1e-5
NEG_INF = -1e30
DEEPNORM_ALPHA = (2 * DEPTH) ** 0.25
DEEPNORM_BETA = (8 * DEPTH) ** -0.25

kernel_name = 'hybrid_na_hyena_swa_diffusion_step'


def _layer_norm(x, g, b):
    xf = x.astype(jnp.float32)
    mu = jnp.mean(xf, -1, keepdims=True)
    var = jnp.mean(jnp.square(xf - mu), -1, keepdims=True)
    return ((xf - mu) * lax.rsqrt(var + LN_EPS) * g.astype(jnp.float32) + b.astype(jnp.float32)).astype(x.dtype)


def _adaln(cond, w, b):
    return jnp.split(jax.nn.silu(cond) @ w + b, 6, axis=-1)


def _split_proj(p):
    sizes = (NA_W, NA_W, NA_W, 3 * HY_WIDTH, SWA_W, SWA_KV_W, SWA_KV_W)
    out = []
    start = 0
    for s in sizes:
        out.append(p[..., start:start + s])
        start += s
    return out


def _axial_rope(x):
    S = x.shape[1]
    t = jnp.arange(S)
    half = HEAD_DIM // 2
    inv = ROPE_THETA ** (-jnp.arange(0, half, 2, dtype=jnp.float32) / half)
    bshape = (S,) + (1,) * (x.ndim - 3) + (half // 2,)

    def rot(xh, pos):
        ang = (pos.astype(jnp.float32)[:, None] * inv[None, :]).reshape(bshape)
        cos, sin = jnp.cos(ang), jnp.sin(ang)
        x1, x2 = jnp.split(xh.astype(jnp.float32), 2, -1)
        return jnp.concatenate([x1 * cos - x2 * sin, x1 * sin + x2 * cos], -1)

    xr, xc = jnp.split(x, 2, -1)
    return jnp.concatenate([rot(xr, t // GRID_W), rot(xc, t % GRID_W)], -1).astype(x.dtype)


def _ctx_attention(q, k, v, sink):
    b, L, hk, g, dh = q.shape
    nb = L // CTX_Q_BLOCK
    qb = jnp.moveaxis(q.reshape(b, nb, CTX_Q_BLOCK, hk, g, dh), 1, 0)

    def one_block(qblk):
        s = jnp.einsum('bqhgd,bkhd->bhgqk', qblk, k).astype(jnp.float32) * ATTN_SCALE
        if sink is not None:
            sk = jnp.broadcast_to(sink.astype(jnp.float32).reshape(1, hk, g, 1, 1), s.shape[:-1] + (1,))
            s = jnp.concatenate([sk, s], -1)
        p = jax.nn.softmax(s, -1)
        if sink is not None:
            p = p[..., 1:]
        return jnp.einsum('bhgqk,bkhd->bqhgd', p.astype(v.dtype), v)

    o = lax.map(one_block, qb)
    return jnp.moveaxis(o, 0, 1).reshape(b, L, hk * g * dh)


def _na_latent(q, k, v, ck, cv, rpb):
    b, S, H, dh = q.shape
    rows = S // GRID_W
    wr = min(NA_WIN_ROWS, rows)
    qg = q.reshape(b, rows, GRID_W, H, dh)
    kg = k.reshape(b, rows, GRID_W, H, dh)
    vg = v.reshape(b, rows, GRID_W, H, dh)
    r = jnp.arange(rows)
    band = jnp.clip(r - wr // 2, 0, rows - wr)[:, None] + jnp.arange(wr)[None, :]
    kb = kg[:, band]
    vb = vg[:, band]
    col = jnp.arange(GRID_W)
    cs = jnp.clip(col - NA_WIN_COLS // 2, 0, GRID_W - NA_WIN_COLS)
    in_win = (col[None, :] >= cs[:, None]) & (col[None, :] < cs[:, None] + NA_WIN_COLS)
    dr = band - r[:, None] + NA_WIN_ROWS - 1
    dc = jnp.clip(col[None, :] - col[:, None], -(NA_WIN_COLS - 1), NA_WIN_COLS - 1) + NA_WIN_COLS - 1
    bias = rpb.astype(jnp.float32)[:, dr[:, None, :, None], dc[None, :, None, :]]
    s_win = jnp.einsum('brchd,brwkhd->bhrcwk', qg, kb).astype(jnp.float32) * ATTN_SCALE + bias[None]
    s_win = jnp.where(in_win[:, None, :], s_win, NEG_INF)
    s_ctx = jnp.einsum('brchd,bjhd->bhrcj', qg, ck).astype(jnp.float32) * ATTN_SCALE
    nw = wr * GRID_W
    p = jax.nn.softmax(jnp.concatenate([s_win.reshape(b, H, rows, GRID_W, nw), s_ctx], -1), -1)
    p_win = p[..., :nw].reshape(b, H, rows, GRID_W, wr, GRID_W).astype(v.dtype)
    p_ctx = p[..., nw:].astype(v.dtype)
    o = jnp.einsum('bhrcwk,brwkhd->brchd', p_win, vb) + jnp.einsum('bhrcj,bjhd->brchd', p_ctx, cv)
    return o.reshape(b, S, H * dh)


def _swa_latent(q, k, v, ck, cv, sink):
    b, S, hk, g, dh = q.shape
    bl = SWA_BLOCK
    nb = S // bl
    pad = ((0, 0), (bl, bl), (0, 0), (0, 0))
    kp = jnp.pad(k, pad)
    vp = jnp.pad(v, pad)

    def bands(xp):
        return jnp.concatenate([xp[:, j * bl:j * bl + S].reshape(b, nb, bl, hk, dh) for j in range(3)], axis=2)

    kb, vb = bands(kp), bands(vp)
    qb = q.reshape(b, nb, bl, hk, g, dh)
    s_win = jnp.einsum('bnqhgd,bnkhd->bhgnqk', qb, kb).astype(jnp.float32) * ATTN_SCALE
    qpos = jnp.arange(S).reshape(nb, bl)
    kpos = (jnp.arange(nb) * bl - bl)[:, None] + jnp.arange(3 * bl)[None, :]
    valid = ((jnp.abs(qpos[:, :, None] - kpos[:, None, :]) <= SWA_WINDOW)
             & (kpos >= 0)[:, None, :] & (kpos < S)[:, None, :])
    s_win = jnp.where(valid, s_win, NEG_INF)
    s_ctx = jnp.einsum('bnqhgd,bjhd->bhgnqj', qb, ck).astype(jnp.float32) * ATTN_SCALE
    s_sink = jnp.broadcast_to(sink.astype(jnp.float32).reshape(1, hk, g, 1, 1, 1), s_win.shape[:-1] + (1,))
    p = jax.nn.softmax(jnp.concatenate([s_sink, s_win, s_ctx], -1), -1)
    p_win = p[..., 1:1 + 3 * bl].astype(v.dtype)
    p_ctx = p[..., 1 + 3 * bl:].astype(v.dtype)
    o = jnp.einsum('bhgnqk,bnkhd->bnqhgd', p_win, vb) + jnp.einsum('bhgnqj,bjhd->bnqhgd', p_ctx, cv)
    return o.reshape(b, S, hk * g * dh)


def _hyena_filter_fft(L, w1, b1, f1, w2, b2, f2, w3):
    f32 = jnp.float32
    pos = jnp.arange(L, dtype=f32)
    t = pos / max(L - 1, 1)
    bands = jnp.linspace(1e-4, HY_POS_BANDS - 1, HY_POS_BANDS, dtype=f32)
    ang = (2.0 * math.pi / L) * pos[:, None] * bands[None, :]
    feat = jnp.concatenate([t[:, None], jnp.cos(ang), -jnp.sin(ang)], -1)
    hdn = jnp.sin(f1.astype(f32) * (feat @ w1.astype(f32) + b1.astype(f32)))
    hdn = jnp.sin(f2.astype(f32) * (hdn @ w2.astype(f32) + b2.astype(f32)))
    h = (hdn @ w3.astype(f32)).reshape(L, 2, HY_ORDER, HY_WIDTH)
    deltas = jnp.abs(jnp.linspace(math.log(HY_DECAY_TARGET) / HY_SLOW_DECAY,
                                  math.log(HY_DECAY_TARGET) / HY_FAST_DECAY, HY_WIDTH, dtype=f32))
    window = jnp.exp(-t[:, None] * deltas[None, :]) + HY_MOD_SHIFT
    h = h * window[:, None, None, :]
    h = h / jnp.sum(jnp.abs(h), axis=(0, 1), keepdims=True)
    k = jnp.concatenate([h[:, 0], jnp.zeros((1, HY_ORDER, HY_WIDTH), f32), jnp.flip(h[1:, 1], axis=0)], axis=0)
    return jnp.fft.rfft(k, axis=0)


def _long_conv(z, kf):
    L = z.shape[1]
    zf = jnp.fft.rfft(z.astype(jnp.float32), n=2 * L, axis=1)
    return jnp.fft.irfft(zf * kf[None], n=2 * L, axis=1)[:, :L].astype(z.dtype)


def _hyena(u, short_w, short_b, w1, b1, f1, w2, b2, f2, w3, skip):
    L = u.shape[1]
    hw = HY_SHORT // 2
    up = jnp.pad(u, ((0, 0), (hw, hw), (0, 0)))
    acc = up[:, 0:L] * short_w[0]
    for j in range(1, HY_SHORT):
        acc = acc + up[:, j:j + L] * short_w[j]
    u = acc + short_b
    v, x1, x2 = jnp.split(u, 3, axis=-1)
    kf = _hyena_filter_fft(L, w1, b1, f1, w2, b2, f2, w3)
    z = v
    for o, gate in enumerate((x1, x2)):
        z = gate * (_long_conv(z, kf[:, o]) + skip[o] * z)
    return z


def _swiglu(h, wg, wu, wd):
    return (jax.nn.silu(h @ wg) * (h @ wu)) @ wd


def _moe(h, router, wg, wu, wd):
    logits = (h @ router).astype(jnp.float32)
    top_v, top_i = lax.top_k(logits, TOP_K)
    gates = jax.nn.softmax(top_v, -1)
    combine = jnp.sum(jax.nn.one_hot(top_i, N_EXPERTS, dtype=jnp.float32) * gates[..., None], axis=-2)
    y = jnp.zeros(h.shape, jnp.float32)
    for e in range(N_EXPERTS):
        y = y + combine[..., e:e + 1] * _swiglu(h, wg[e], wu[e], wd[e]).astype(jnp.float32)
    return y.astype(h.dtype)


def _channel_mixer(h, l, ffn_w_gate, ffn_w_up, ffn_w_down, moe_router, moe_w_gate, moe_w_up, moe_w_down):
    i = l // 2
    if l % 2 == 0:
        return _swiglu(h, ffn_w_gate[i], ffn_w_up[i], ffn_w_down[i])
    return _moe(h, moe_router[i], moe_w_gate[i], moe_w_up[i], moe_w_down[i])


def setup_inputs(seed: int = 0) -> dict:
    key = jax.random.key(seed)
    ks = iter(jax.random.split(key, 48))
    f32 = jnp.float32

    def nrm(shape, scale=1.0):
        return jax.random.normal(next(ks), shape, f32) * scale

    D = D_MODEL
    beta = DEEPNORM_BETA
    return {
        'x_prompt': nrm((BATCH, SEQ, D)),
        'x_sample': nrm((DEC_BATCH, DEC_SEQ, D)),
        'cache_na_k': nrm((DEC_BATCH, DEPTH, PAST_LEN, NA_HEADS, HEAD_DIM)),
        'cache_na_v': nrm((DEC_BATCH, DEPTH, PAST_LEN, NA_HEADS, HEAD_DIM)),
        'cache_swa_k': nrm((DEC_BATCH, DEPTH, PAST_LEN, SWA_KV_HEADS, HEAD_DIM)),
        'cache_swa_v': nrm((DEC_BATCH, DEPTH, PAST_LEN, SWA_KV_HEADS, HEAD_DIM)),
        'c': nrm((DEC_BATCH, D)),
        'c_ctx': nrm((D,)),
        'w_mod': nrm((DEPTH, D, 6 * D), 0.5 * D ** -0.5),
        'b_mod': nrm((DEPTH, 6 * D), 0.02),
        'w_in': nrm((DEPTH, D, IN_W), D ** -0.5),
        'w_out': nrm((DEPTH, MIX_W, D), beta * MIX_W ** -0.5),
        'na_rpb': nrm((DEPTH, NA_HEADS, 2 * NA_WIN_ROWS - 1, 2 * NA_WIN_COLS - 1), 0.2),
        'swa_sink': nrm((DEPTH, SWA_HEADS)),
        'hy_short_w': nrm((DEPTH, HY_SHORT, 3 * HY_WIDTH), HY_SHORT ** -0.5),
        'hy_short_b': nrm((DEPTH, 3 * HY_WIDTH), 0.02),
        'hy_w1': nrm((DEPTH, HY_POS_DIM, HY_FFN), HY_POS_DIM ** -0.5),
        'hy_b1': nrm((DEPTH, HY_FFN), 0.02),
        'hy_freq1': 1.0 + nrm((DEPTH, HY_FFN), 0.01),
        'hy_w2': nrm((DEPTH, HY_FFN, HY_FFN), HY_FFN ** -0.5),
        'hy_b2': nrm((DEPTH, HY_FFN), 0.02),
        'hy_freq2': 1.0 + nrm((DEPTH, HY_FFN), 0.01),
        'hy_w3': nrm((DEPTH, HY_FFN, 2 * HY_ORDER * HY_WIDTH), HY_FFN ** -0.5),
        'hy_skip': nrm((DEPTH, HY_ORDER, HY_WIDTH), 0.5),
        'ln1_g': 1.0 + nrm((DEPTH, D), 0.02),
        'ln1_b': nrm((DEPTH, D), 0.02),
        'ln2_g': 1.0 + nrm((DEPTH, D), 0.02),
        'ln2_b': nrm((DEPTH, D), 0.02),
        'ffn_w_gate': nrm((N_DENSE, D, D_FF), D ** -0.5),
        'ffn_w_up': nrm((N_DENSE, D, D_FF), D ** -0.5),
        'ffn_w_down': nrm((N_DENSE, D_FF, D), beta * D_FF ** -0.5),
        'moe_router': nrm((N_MOE, D, N_EXPERTS), D ** -0.5),
        'moe_w_gate': nrm((N_MOE, N_EXPERTS, D, D_FF_EXPERT), D ** -0.5),
        'moe_w_up': nrm((N_MOE, N_EXPERTS, D, D_FF_EXPERT), D ** -0.5),
        'moe_w_down': nrm((N_MOE, N_EXPERTS, D_FF_EXPERT, D), beta * D_FF_EXPERT ** -0.5),
    }


def reference(x_prompt, x_sample, cache_na_k, cache_na_v, cache_swa_k, cache_swa_v, c, c_ctx,
              w_mod, b_mod, w_in, w_out, na_rpb, swa_sink,
              hy_short_w, hy_short_b, hy_w1, hy_b1, hy_freq1, hy_w2, hy_b2, hy_freq2, hy_w3, hy_skip,
              ln1_g, ln1_b, ln2_g, ln2_b, ffn_w_gate, ffn_w_up, ffn_w_down,
              moe_router, moe_w_gate, moe_w_up, moe_w_down):
    xp, xs = x_prompt, x_sample
    na_k_list, na_v_list, swa_k_list, swa_v_list = [], [], [], []
    for l in range(DEPTH):
        hy = (hy_short_w[l], hy_short_b[l], hy_w1[l], hy_b1[l], hy_freq1[l],
              hy_w2[l], hy_b2[l], hy_freq2[l], hy_w3[l], hy_skip[l])
        sink_l = swa_sink[l].reshape(SWA_KV_HEADS, SWA_GROUP)
        ffn_args = (ffn_w_gate, ffn_w_up, ffn_w_down, moe_router, moe_w_gate, moe_w_up, moe_w_down)

        sh1, sc1, g1, sh2, sc2, g2 = _adaln(c_ctx, w_mod[l], b_mod[l])
        bp, lp = xp.shape[:2]
        qa, ka, va, hin, qc, kc, vc = _split_proj((xp * (1 + sc1) + sh1) @ w_in[l])
        ka = ka.reshape(bp, lp, NA_HEADS, HEAD_DIM)
        va = va.reshape(bp, lp, NA_HEADS, HEAD_DIM)
        kc = kc.reshape(bp, lp, SWA_KV_HEADS, HEAD_DIM)
        vc = vc.reshape(bp, lp, SWA_KV_HEADS, HEAD_DIM)
        o_a = _ctx_attention(qa.reshape(bp, lp, NA_HEADS, 1, HEAD_DIM), ka, va, None)
        o_b = _hyena(hin, *hy)
        o_c = _ctx_attention(qc.reshape(bp, lp, SWA_KV_HEADS, SWA_GROUP, HEAD_DIM), kc, vc, sink_l)
        mix = jnp.concatenate([o_a, o_b, o_c], -1) @ w_out[l]
        xp = _layer_norm(DEEPNORM_ALPHA * xp + g1 * mix, ln1_g[l], ln1_b[l])
        xp = _layer_norm(DEEPNORM_ALPHA * xp + g2 * _channel_mixer(xp * (1 + sc2) + sh2, l, *ffn_args),
                         ln2_g[l], ln2_b[l])
        na_k_list.append(ka)
        na_v_list.append(va)
        swa_k_list.append(kc)
        swa_v_list.append(vc)

        sh1, sc1, g1, sh2, sc2, g2 = [m[:, None, :] for m in _adaln(c, w_mod[l], b_mod[l])]
        bs, ls = xs.shape[:2]
        qa, ka, va, hin, qc, kc, vc = _split_proj((xs * (1 + sc1) + sh1) @ w_in[l])
        o_a = _na_latent(qa.reshape(bs, ls, NA_HEADS, HEAD_DIM), ka.reshape(bs, ls, NA_HEADS, HEAD_DIM),
                         va.reshape(bs, ls, NA_HEADS, HEAD_DIM), cache_na_k[:, l], cache_na_v[:, l], na_rpb[l])
        o_b = _hyena(hin, *hy)
        qc_r = _axial_rope(qc.reshape(bs, ls, SWA_KV_HEADS, SWA_GROUP, HEAD_DIM))
        kc_r = _axial_rope(kc.reshape(bs, ls, SWA_KV_HEADS, HEAD_DIM))
        o_c = _swa_latent(qc_r, kc_r, vc.reshape(bs, ls, SWA_KV_HEADS, HEAD_DIM),
                          cache_swa_k[:, l], cache_swa_v[:, l], sink_l)
        mix = jnp.concatenate([o_a, o_b, o_c], -1) @ w_out[l]
        xs = _layer_norm(DEEPNORM_ALPHA * xs + g1 * mix, ln1_g[l], ln1_b[l])
        xs = _layer_norm(DEEPNORM_ALPHA * xs + g2 * _channel_mixer(xs * (1 + sc2) + sh2, l, *ffn_args),
                         ln2_g[l], ln2_b[l])

    new_na_k = jnp.stack(na_k_list, axis=1)
    new_na_v = jnp.stack(na_v_list, axis=1)
    new_swa_k = jnp.stack(swa_k_list, axis=1)
    new_swa_v = jnp.stack(swa_v_list, axis=1)
    return (xp, xs, new_na_k, new_na_v, new_swa_k, new_swa_v)
```

```python
import functools
import math

import numpy as np
import jax
import jax.numpy as jnp
from jax import lax
from jax.experimental import pallas as pl
from jax.experimental.pallas import tpu as pltpu

F32 = jnp.float32
BF16 = jnp.bfloat16

D_MODEL = 2048
BATCH = 32
SEQ = 256
DEPTH = 2
DEC_BATCH = 2
DEC_SEQ = 2048
PAST_LEN = 256
GRID_W = 64
GRID_ROWS = DEC_SEQ // GRID_W
HEAD_DIM = 64
ATTN_SCALE = HEAD_DIM ** -0.5
NA_HEADS = 8
NA_WIN_ROWS = 8
NA_WIN_COLS = 16
HY_WIDTH = 1024
HY_ORDER = 2
HY_SHORT = 3
HY_POS_BANDS = 16
HY_POS_DIM = 1 + 2 * HY_POS_BANDS
HY_FFN = 64
HY_FAST_DECAY = 0.3
HY_SLOW_DECAY = 1.5
HY_DECAY_TARGET = 1e-2
HY_MOD_SHIFT = 0.05
SWA_HEADS = 8
SWA_KV_HEADS = 2
SWA_GROUP = SWA_HEADS // SWA_KV_HEADS
SWA_WINDOW = 128
SWA_BLOCK = 128
NA_W = NA_HEADS * HEAD_DIM
SWA_W = SWA_HEADS * HEAD_DIM
SWA_KV_W = SWA_KV_HEADS * HEAD_DIM
MIX_W = NA_W + HY_WIDTH + SWA_W
IN_W = 3 * NA_W + 3 * HY_WIDTH + SWA_W + 2 * SWA_KV_W
D_FF = 7168
N_EXPERTS = 8
ROPE_THETA = 10000.0
LN_EPS = 1e-5
NEG_INF = -1e30
DEEPNORM_ALPHA = (2 * DEPTH) ** 0.25

NP = BATCH * SEQ
NS = DEC_BATCH * DEC_SEQ
NT = NP + NS
N_COND = 1 + DEC_BATCH
COND_PAD = 8

COL_QA, COL_KA, COL_VA = 0, NA_W, 2 * NA_W
COL_HY = 3 * NA_W
COL_QC = COL_HY + 3 * HY_WIDTH
COL_KC = COL_QC + SWA_W
COL_VC = COL_KC + SWA_KV_W

HY_BLK = 256
HY_NFFT = 2 * HY_BLK

LANES = 128
V7X_VMEM_BYTES = 64 * 1024 * 1024
VMEM_LIMIT_BYTES = V7X_VMEM_BYTES * 7 // 8


def _params(grid_rank):
    return pltpu.CompilerParams(dimension_semantics=("arbitrary",) * grid_rank,
                                vmem_limit_bytes=VMEM_LIMIT_BYTES)


def _cond_of_row(row):
    return jnp.maximum((row - NP) // DEC_SEQ + 1, 0)


def _bf16_split_np(a):
    a = np.asarray(a, np.float32)
    hi = a.astype(BF16)
    lo = (a - hi.astype(np.float32)).astype(BF16)
    return hi, lo


_MOD_TN = 512


def _mod_kernel(cb_ref, w_ref, b_ref, o_ref, s_scr):
    @pl.when((pl.program_id(0) == 0) & (pl.program_id(1) == 0))
    def _():
        c = cb_ref[...]
        s_scr[...] = c * jax.nn.sigmoid(c)

    o_ref[...] = jnp.zeros_like(o_ref)
    for n in range(_MOD_TN // LANES):
        sl = slice(n * LANES, (n + 1) * LANES)
        w = w_ref[:, sl]
        for j in range(N_COND):
            acc = jnp.sum(w * s_scr[j], axis=0, keepdims=True)
            o_ref[j:j + 1, sl] = acc + b_ref[:, sl]


def _modulation(cond, w_mod, b_mod):
    cb = jnp.broadcast_to(cond[:, :, None], (N_COND, D_MODEL, LANES))
    n6 = 6 * D_MODEL
    out = pl.pallas_call(
        _mod_kernel,
        grid=(DEPTH, n6 // _MOD_TN),
        in_specs=[
            pl.BlockSpec((N_COND, D_MODEL, LANES), lambda l, j: (0, 0, 0)),
            pl.BlockSpec((None, D_MODEL, _MOD_TN), lambda l, j: (l, 0, j)),
            pl.BlockSpec((None, 1, _MOD_TN), lambda l, j: (l, 0, j)),
        ],
        out_specs=pl.BlockSpec((None, COND_PAD, _MOD_TN), lambda l, j: (l, 0, j)),
        out_shape=jax.ShapeDtypeStruct((DEPTH, COND_PAD, n6), F32),
        scratch_shapes=[pltpu.VMEM((N_COND, D_MODEL, LANES), F32)],
        compiler_params=_params(2),
        name="adaln_mod",
    )(cb, w_mod, b_mod.reshape(DEPTH, 1, n6))
    out = out.reshape(DEPTH, COND_PAD, 6, D_MODEL).transpose(0, 2, 1, 3)
    return out.reshape(DEPTH, 6, COND_PAD, 1, D_MODEL)


def _mod_spec(layer, which, tm):
    return pl.BlockSpec((None, None, None, 1, D_MODEL),
                        lambda i, *_: (layer, which, _cond_of_row(i * tm), 0, 0))


_PRO_TM = 512


def _prologue_kernel(xp_ref, xs_ref, sc_ref, sh_ref, x_ref, xm_ref):
    i = pl.program_id(0)

    def emit(x):
        x_ref[...] = x
        xm_ref[...] = (x * (1.0 + sc_ref[...]) + sh_ref[...]).astype(BF16)

    @pl.when(i < NP // _PRO_TM)
    def _():
        emit(xp_ref[...])

    @pl.when(i >= NP // _PRO_TM)
    def _():
        emit(xs_ref[...])


def _prologue(xp, xs, mod):
    tm = _PRO_TM
    npt = NP // tm
    return pl.pallas_call(
        _prologue_kernel,
        grid=(NT // tm,),
        in_specs=[
            pl.BlockSpec((tm, D_MODEL), lambda i: (jnp.minimum(i, npt - 1), 0)),
            pl.BlockSpec((tm, D_MODEL), lambda i: (jnp.maximum(i - npt, 0), 0)),
            _mod_spec(0, 1, tm),
            _mod_spec(0, 0, tm),
        ],
        out_specs=[pl.BlockSpec((tm, D_MODEL), lambda i: (i, 0)),
                   pl.BlockSpec((tm, D_MODEL), lambda i: (i, 0))],
        out_shape=[jax.ShapeDtypeStruct((NT, D_MODEL), F32),
                   jax.ShapeDtypeStruct((NT, D_MODEL), BF16)],
        compiler_params=_params(1),
        name="prologue",
    )(xp, xs, mod, mod)


_INP_TM = 1024
_INP_TN = 768


def _inproj_kernel(x_ref, w_ref, o_ref):
    o_ref[...] = jnp.dot(x_ref[...], w_ref[...].astype(BF16), preferred_element_type=F32)


def _inproj(xm, w_in, layer):
    tm, tn = _INP_TM, _INP_TN
    return pl.pallas_call(
        _inproj_kernel,
        grid=(NT // tm, IN_W // tn),
        in_specs=[
            pl.BlockSpec((tm, D_MODEL), lambda i, j: (i, 0)),
            pl.BlockSpec((None, D_MODEL, tn), lambda i, j: (layer, 0, j)),
        ],
        out_specs=pl.BlockSpec((tm, tn), lambda i, j: (i, j)),
        out_shape=jax.ShapeDtypeStruct((NT, IN_W), F32),
        compiler_params=_params(2),
        name="inproj",
    )(xm, w_in)


def _attend(q, k, v, bias=None, valid=None, sink=None):
    s = lax.dot_general(q, k, (((1,), (1,)), ((), ())), preferred_element_type=F32)
    if bias is not None:
        s = s + bias
    if valid is not None:
        s = jnp.where(valid, s, NEG_INF)
    m = jnp.max(s, axis=-1, keepdims=True)
    if sink is not None:
        m = jnp.maximum(m, sink)
    p = jnp.exp(s - m)
    denom = jnp.sum(p, axis=-1, keepdims=True)
    if sink is not None:
        denom = denom + jnp.exp(sink - m)
    o = jnp.dot(p.astype(BF16), v, preferred_element_type=F32)
    return o / denom


def _head(x, h):
    return x[:, h * HEAD_DIM:(h + 1) * HEAD_DIM]


def _rope(x, cos, sin_signed):
    width = x.shape[1]
    lane = lax.broadcasted_iota(jnp.int32, x.shape, 1)
    first = (lane % 32) < 16
    partner = jnp.where(first, pltpu.roll(x, width - 16, 1), pltpu.roll(x, 16, 1))
    return x * cos + partner * sin_signed


def _rope_tables(n_heads):
    half = HEAD_DIM // 2
    inv = ROPE_THETA ** (-np.arange(0, half, 2, dtype=np.float64) / half)
    t = np.arange(DEC_SEQ)
    inv = inv.astype(np.float32).astype(np.float64)
    ang_r = (t // GRID_W)[:, None].astype(np.float64) * inv[None, :]
    ang_c = (t % GRID_W)[:, None].astype(np.float64) * inv[None, :]
    cos = np.concatenate([np.cos(ang_r), np.cos(ang_r), np.cos(ang_c), np.cos(ang_c)], -1)
    sin = np.concatenate([-np.sin(ang_r), np.sin(ang_r), -np.sin(ang_c), np.sin(ang_c)], -1)
    cos = np.tile(cos, (1, n_heads)).astype(np.float32)
    sin = np.tile(sin, (1, n_heads)).astype(np.float32)
    return cos, sin


def _ctx_attn_kernel(layer, sink_ref, qa_ref, ka_ref, va_ref, qc_ref, kvc_ref, oa_ref, oc_ref):
    qa = (qa_ref[...] * ATTN_SCALE).astype(BF16)
    ka = ka_ref[...].astype(BF16)
    va = va_ref[...].astype(BF16)
    outs = [_attend(_head(qa, h), _head(ka, h), _head(va, h)) for h in range(NA_HEADS)]
    oa_ref[...] = jnp.concatenate(outs, axis=-1).astype(BF16)

    qc = (qc_ref[...] * ATTN_SCALE).astype(BF16)
    kvc = kvc_ref[...].astype(BF16)
    kc, vc = kvc[:, :SWA_KV_W], kvc[:, SWA_KV_W:]
    outs = []
    for h in range(SWA_HEADS):
        hk = h // SWA_GROUP
        outs.append(_attend(_head(qc, h), _head(kc, hk), _head(vc, hk), sink=sink_ref[layer, h]))
    oc_ref[...] = jnp.concatenate(outs, axis=-1).astype(BF16)


def _ctx_attention(proj, swa_sink, layer):
    blk = lambda w, c: pl.BlockSpec((SEQ, w), lambda b: (b, c // w))
    return pl.pallas_call(
        functools.partial(_ctx_attn_kernel, layer),
        grid=(BATCH,),
        in_specs=[
            pl.BlockSpec(memory_space=pltpu.SMEM),
            blk(NA_W, COL_QA), blk(NA_W, COL_KA), blk(NA_W, COL_VA),
            blk(SWA_W, COL_QC), blk(2 * SWA_KV_W, COL_KC),
        ],
        out_specs=[pl.BlockSpec((SEQ, NA_W), lambda b: (b, 0)),
                   pl.BlockSpec((SEQ, SWA_W), lambda b: (b, 0))],
        out_shape=[jax.ShapeDtypeStruct((NT, NA_W), BF16),
                   jax.ShapeDtypeStruct((NT, SWA_W), BF16)],
        compiler_params=_params(1),
        name="ctx_attn",
    )(swa_sink, proj, proj, proj, proj, proj)


_NA_BAND = NA_WIN_ROWS * GRID_W


def _na_band_start(r):
    return jnp.clip(r - NA_WIN_ROWS // 2, 0, GRID_ROWS - NA_WIN_ROWS)


def _na_kernel(q_ref, k_ref, v_ref, ck_ref, cv_ref, bias_ref, prev_ref, o_ref):
    del prev_ref
    r = pl.program_id(1)
    start = pl.multiple_of(_na_band_start(r) * GRID_W, GRID_W)
    q = (q_ref[...] * ATTN_SCALE).astype(BF16)
    k = jnp.concatenate([k_ref[pl.ds(start, _NA_BAND), :].astype(BF16), ck_ref[...].astype(BF16)], axis=0)
    v = jnp.concatenate([v_ref[pl.ds(start, _NA_BAND), :].astype(BF16), cv_ref[...].astype(BF16)], axis=0)
    outs = [_attend(_head(q, h), _head(k, h), _head(v, h), bias=bias_ref[h]) for h in range(NA_HEADS)]
    o_ref[...] = jnp.concatenate(outs, axis=-1).astype(BF16)


def _na_bias_tables(rpb_l):
    nc = 2 * NA_WIN_COLS - 1
    col = np.arange(GRID_W)
    dc = np.clip(col[None, :] - col[:, None], -(NA_WIN_COLS - 1), NA_WIN_COLS - 1) + NA_WIN_COLS - 1
    onehot = (dc.reshape(-1)[None, :] == np.arange(nc)[:, None]).astype(np.float32)
    cs = np.clip(col - NA_WIN_COLS // 2, 0, GRID_W - NA_WIN_COLS)
    in_win = (col[None, :] >= cs[:, None]) & (col[None, :] < cs[:, None] + NA_WIN_COLS)
    toe = jnp.einsum('hrd,dn->hrn', rpb_l.astype(F32), jnp.asarray(onehot),
                     precision=lax.Precision.HIGHEST)
    toe = toe.reshape(NA_HEADS, 2 * NA_WIN_ROWS - 1, GRID_W, GRID_W)
    toe = jnp.where(jnp.asarray(in_win)[None, None], toe, NEG_INF)
    strips = jnp.stack([toe[:, o:o + NA_WIN_ROWS] for o in range(NA_WIN_ROWS)], 0)
    strips = strips.transpose(0, 1, 3, 2, 4).reshape(NA_WIN_ROWS, NA_HEADS, GRID_W, _NA_BAND)
    return jnp.pad(strips, ((0, 0), (0, 0), (0, 0), (0, PAST_LEN)))


def _na_latent(proj, cache_k, cache_v, rpb_l, o_prev, layer):
    bias = _na_bias_tables(rpb_l)
    row0 = NP // GRID_W
    qmap = lambda b, r: (row0 + b * GRID_ROWS + r, 0)
    kvspec = lambda c: pl.BlockSpec((DEC_SEQ, NA_W), lambda b, r: (NP // DEC_SEQ + b, c // NA_W))
    cspec = pl.BlockSpec((None, None, PAST_LEN, NA_W), lambda b, r: (b, layer, 0, 0))

    def bias_map(b, r):
        return (_na_band_start(r) - r + NA_WIN_ROWS - 1, 0, 0, 0)

    return pl.pallas_call(
        _na_kernel,
        grid=(DEC_BATCH, GRID_ROWS),
        in_specs=[
            pl.BlockSpec((GRID_W, NA_W), qmap),
            kvspec(COL_KA), kvspec(COL_VA), cspec, cspec,
            pl.BlockSpec((None, NA_HEADS, GRID_W, _NA_BAND + PAST_LEN), bias_map),
            pl.BlockSpec(memory_space=pl.ANY),
        ],
        out_specs=pl.BlockSpec((GRID_W, NA_W), qmap),
        out_shape=jax.ShapeDtypeStruct((NT, NA_W), BF16),
        input_output_aliases={6: 0},
        compiler_params=_params(2),
        name="na_latent",
    )(proj, proj, proj, cache_k.reshape(DEC_BATCH, DEPTH, PAST_LEN, NA_W),
      cache_v.reshape(DEC_BATCH, DEPTH, PAST_LEN, NA_W), bias, o_prev)


_SWA_KEYS = 3 * SWA_BLOCK


def _swa_kernel(layer, sink_ref, q_ref, kv_ref, ck_ref, cv_ref, cq_ref, sq_ref, ckt_ref, skt_ref,
                prev_ref, o_ref):
    del prev_ref
    n = pl.program_id(1)
    start = pl.multiple_of(jnp.clip((n - 1) * SWA_BLOCK, 0, DEC_SEQ - _SWA_KEYS), SWA_BLOCK)
    q = (_rope(q_ref[...], cq_ref[...], sq_ref[...]) * ATTN_SCALE).astype(BF16)
    kv = kv_ref[pl.ds(start, _SWA_KEYS), :]
    kw = _rope(kv[:, :SWA_KV_W], ckt_ref[pl.ds(start, _SWA_KEYS), :], skt_ref[pl.ds(start, _SWA_KEYS), :])
    k = jnp.concatenate([kw.astype(BF16), ck_ref[...].astype(BF16)], axis=0)
    v = jnp.concatenate([kv[:, SWA_KV_W:].astype(BF16), cv_ref[...].astype(BF16)], axis=0)
    nk = _SWA_KEYS + PAST_LEN
    qpos = n * SWA_BLOCK + lax.broadcasted_iota(jnp.int32, (SWA_BLOCK, nk), 0)
    kidx = lax.broadcasted_iota(jnp.int32, (SWA_BLOCK, nk), 1)
    valid = (jnp.abs(qpos - (start + kidx)) <= SWA_WINDOW) | (kidx >= _SWA_KEYS)
    outs = []
    for h in range(SWA_HEADS):
        hk = h // SWA_GROUP
        outs.append(_attend(_head(q, h), _head(k, hk), _head(v, hk), valid=valid, sink=sink_ref[layer, h]))
    o_ref[...] = jnp.concatenate(outs, axis=-1).astype(BF16)


def _swa_latent(proj, cache_k, cache_v, swa_sink, o_prev, layer):
    cq, sq = _rope_tables(SWA_HEADS)
    ck, sk = _rope_tables(SWA_KV_HEADS)
    row0 = NP // SWA_BLOCK
    nblk = DEC_SEQ // SWA_BLOCK
    qmap = lambda b, n: (row0 + b * nblk + n, COL_QC // SWA_W)
    omap = lambda b, n: (row0 + b * nblk + n, 0)
    cspec = pl.BlockSpec((None, None, PAST_LEN, SWA_KV_W), lambda b, n: (b, layer, 0, 0))
    tq = pl.BlockSpec((SWA_BLOCK, SWA_W), lambda b, n: (n, 0))
    tk = pl.BlockSpec((DEC_SEQ, SWA_KV_W), lambda b, n: (0, 0))
    return pl.pallas_call(
        functools.partial(_swa_kernel, layer),
        grid=(DEC_BATCH, nblk),
        in_specs=[
            pl.BlockSpec(memory_space=pltpu.SMEM),
            pl.BlockSpec((SWA_BLOCK, SWA_W), qmap),
            pl.BlockSpec((DEC_SEQ, 2 * SWA_KV_W), lambda b, n: (NP // DEC_SEQ + b, COL_KC // (2 * SWA_KV_W))),
            cspec, cspec, tq, tq, tk, tk,
            pl.BlockSpec(memory_space=pl.ANY),
        ],
        out_specs=pl.BlockSpec((SWA_BLOCK, SWA_W), omap),
        out_shape=jax.ShapeDtypeStruct((NT, SWA_W), BF16),
        input_output_aliases={9: 0},
        compiler_params=_params(2),
        name="swa_latent",
    )(swa_sink, proj, proj, cache_k.reshape(DEC_BATCH, DEPTH, PAST_LEN, SWA_KV_W),
      cache_v.reshape(DEC_BATCH, DEPTH, PAST_LEN, SWA_KV_W),
      jnp.asarray(cq), jnp.asarray(sq), jnp.asarray(ck), jnp.asarray(sk), o_prev)


def _dft_mats():
    bk, n = HY_BLK, HY_NFFT
    s = np.arange(bk)
    ang = 2.0 * np.pi * ((s[:, None] * s[None, :]) % n) / n
    fwd = np.zeros((n, bk))
    fwd[:bk] = np.cos(ang)
    fwd[bk:] = -np.sin(ang)
    fwd[bk] = (-1.0) ** s
    inv = np.zeros((bk, n))
    inv[:, :bk] = (2.0 / n) * np.cos(ang)
    inv[:, 0] = 1.0 / n
    inv[:, bk:] = -(2.0 / n) * np.sin(ang)
    inv[:, bk] = (1.0 / n) * (-1.0) ** s
    fh, fl = _bf16_split_np(fwd)
    ih, il = _bf16_split_np(inv)
    return np.concatenate([fh, fh, fl], axis=1), np.concatenate([ih, ih, il], axis=1)


def _split3(x):
    hi = x.astype(BF16)
    lo = (x - hi.astype(F32)).astype(BF16)
    return jnp.concatenate([hi, lo, hi], axis=0)


def _hy_features(L):
    pos = np.arange(L, dtype=np.float64)
    t = (pos.astype(np.float32) / np.float32(max(L - 1, 1))).astype(np.float64)
    bands = np.linspace(1e-4, HY_POS_BANDS - 1, HY_POS_BANDS, dtype=np.float32).astype(np.float64)
    ang = np.float64(np.float32(2.0 * math.pi / L)) * pos[:, None] * bands[None, :]
    feat = np.concatenate([t[:, None], np.cos(ang), -np.sin(ang)], -1)
    feat = np.pad(feat, ((0, 0), (0, HY_FFN - HY_POS_DIM)))
    deltas = np.abs(np.linspace(math.log(HY_DECAY_TARGET) / HY_SLOW_DECAY,
                                math.log(HY_DECAY_TARGET) / HY_FAST_DECAY, HY_WIDTH, dtype=np.float32))
    window = np.exp(-t[:, None] * deltas[None, :].astype(np.float64)) + HY_MOD_SHIFT
    return feat.astype(np.float32), window.astype(np.float32)


_HYF_TC = 512


def _hy_filter_kernel(feat_ref, w1_ref, b1_ref, f1_ref, w2_ref, b2_ref, f2_ref, w3f_ref, w3b_ref, win_ref,
                      hf_ref, hb_ref):
    hp = lax.Precision.HIGHEST
    z = jnp.dot(feat_ref[...], w1_ref[...], precision=hp, preferred_element_type=F32) + b1_ref[...]
    h = jnp.sin(f1_ref[...] * z)
    z = jnp.dot(h, w2_ref[...], precision=hp, preferred_element_type=F32) + b2_ref[...]
    h = jnp.sin(f2_ref[...] * z)
    win = win_ref[...]
    af = jnp.dot(h, w3f_ref[...], precision=hp, preferred_element_type=F32) * win
    ab = jnp.dot(h, w3b_ref[...], precision=hp, preferred_element_type=F32) * win
    nrm = jnp.sum(jnp.abs(af), axis=0, keepdims=True) + jnp.sum(jnp.abs(ab), axis=0, keepdims=True)
    hf_ref[...] = af / nrm
    hb_ref[...] = ab / nrm


def _hy_filter(L, layer, hy_w1p, hy_b1, hy_freq1, hy_w2, hy_b2, hy_freq2, hy_w3):
    feat, window = _hy_features(L)
    tc = _HYF_TC
    ncc = HY_WIDTH // tc
    ow = HY_ORDER * HY_WIDTH
    small = lambda shape: pl.BlockSpec((None,) + shape, lambda o, c: (layer,) + (0,) * len(shape))
    w3spec = lambda d: pl.BlockSpec((None, HY_FFN, tc), lambda o, c: (layer, 0, (d * HY_ORDER + o) * ncc + c))
    ospec = pl.BlockSpec((L, tc), lambda o, c: (0, o * ncc + c))
    vec = lambda a: a.reshape(DEPTH, 1, HY_FFN)
    return pl.pallas_call(
        _hy_filter_kernel,
        grid=(HY_ORDER, ncc),
        in_specs=[
            pl.BlockSpec((L, HY_FFN), lambda o, c: (0, 0)),
            small((HY_FFN, HY_FFN)), small((1, HY_FFN)), small((1, HY_FFN)),
            small((HY_FFN, HY_FFN)), small((1, HY_FFN)), small((1, HY_FFN)),
            w3spec(0), w3spec(1),
            pl.BlockSpec((L, tc), lambda o, c: (0, c)),
        ],
        out_specs=[ospec, ospec],
        out_shape=[jax.ShapeDtypeStruct((L, ow), F32), jax.ShapeDtypeStruct((L, ow), F32)],
        compiler_params=_params(2),
        name=f"hy_filter_{L}",
    )(jnp.asarray(feat), hy_w1p, vec(hy_b1), vec(hy_freq1), hy_w2, vec(hy_b2), vec(hy_freq2),
      hy_w3, hy_w3, jnp.asarray(window))


_HYS_TC = 256


def _hy_spectrum_kernel(nb, hf_ref, hb_ref, f3_ref, g_ref):
    n, bk = HY_NFFT, HY_BLK
    row = lax.broadcasted_iota(jnp.int32, (n, 1), 0)
    ones_lo = (row <= bk).astype(F32)
    sgn = jnp.where(row % 2 == 0, 1.0, -1.0).astype(F32)
    conj = jnp.where(row > bk, -1.0, 1.0).astype(F32)
    f3 = f3_ref[...]

    def spectra(h_ref, drop_lag0):
        first, tail = [], []
        for b in range(nb):
            blk = h_ref[b * bk:(b + 1) * bk, :]
            t = jnp.dot(f3, _split3(blk), preferred_element_type=F32)
            head = ones_lo * blk[0:1, :]
            if b == 0 and drop_lag0:
                t = t - head
                tail.append(sgn * t)
            else:
                tail.append(sgn * (t - head))
            first.append(t)
        return [first[d] + (tail[d - 1] if d >= 1 else 0.0) for d in range(nb)]

    gf = spectra(hf_ref, False)
    gb = spectra(hb_ref, True)
    g_ref[nb - 1] = gf[0] + conj * gb[0]
    for d in range(1, nb):
        g_ref[nb - 1 + d] = gf[d]
        g_ref[nb - 1 - d] = conj * gb[d]


def _hy_spectrum(hf, hb, f3):
    L = hf.shape[0]
    nb = L // HY_BLK
    ow = HY_ORDER * HY_WIDTH
    tc = _HYS_TC
    return pl.pallas_call(
        functools.partial(_hy_spectrum_kernel, nb),
        grid=(ow // tc,),
        in_specs=[
            pl.BlockSpec((L, tc), lambda c: (0, c)),
            pl.BlockSpec((L, tc), lambda c: (0, c)),
            pl.BlockSpec((HY_NFFT, 3 * HY_BLK), lambda c: (0, 0)),
        ],
        out_specs=pl.BlockSpec((2 * nb - 1, HY_NFFT, tc), lambda c: (0, 0, c)),
        out_shape=jax.ShapeDtypeStruct((2 * nb - 1, HY_NFFT, ow), F32),
        compiler_params=_params(1),
        name=f"hy_spectrum_{L}",
    )(hf, hb, f3)


def _spec_mul(x, g):
    bk = HY_BLK
    xa, xb = x[:bk], x[bk:]
    ga, gb = g[:bk], g[bk:]
    row0 = lax.broadcasted_iota(jnp.int32, xa.shape, 0) == 0
    bb = xb * gb
    pa = xa * ga - jnp.where(row0, 0.0, bb)
    pb = jnp.where(row0, bb, xa * gb + xb * ga)
    return jnp.concatenate([pa, pb], axis=0)


def _short_conv(u, w, b):
    L = u.shape[0]
    row = lax.broadcasted_iota(jnp.int32, u.shape, 0)
    prev = jnp.where(row == 0, 0.0, pltpu.roll(u, 1, 0))
    nxt = jnp.where(row == L - 1, 0.0, pltpu.roll(u, L - 1, 0))
    return prev * w[0:1] + u * w[1:2] + nxt * w[2:3] + b


def _hy_conv_kernel(nb, uv_ref, u1_ref, u2_ref, wv_ref, w1_ref, w2_ref, bv_ref, b1_ref, b2_ref, skip_ref,
                    g0_ref, g1_ref, f3_ref, i3_ref, *rest):
    o_ref = rest[-1]
    bk = HY_BLK
    v = _short_conv(uv_ref[...], wv_ref[...], bv_ref[...])
    gates = (_short_conv(u1_ref[...], w1_ref[...], b1_ref[...]),
             _short_conv(u2_ref[...], w2_ref[...], b2_ref[...]))
    cc = v.shape[1]
    f3 = f3_ref[...]
    i3 = i3_ref[...]
    z = v
    for o, g_ref in enumerate((g0_ref, g1_ref)):
        zcat = jnp.concatenate([z[j * bk:(j + 1) * bk] for j in range(nb)], axis=1)
        x = jnp.dot(f3, _split3(zcat), preferred_element_type=F32)
        prods = []
        for i in range(nb):
            acc = None
            for j in range(nb):
                t = _spec_mul(x[:, j * cc:(j + 1) * cc], g_ref[i - j + nb - 1])
                acc = t if acc is None else acc + t
            prods.append(acc)
        pcat = jnp.concatenate(prods, axis=1)
        y = jnp.dot(i3, _split3(pcat), preferred_element_type=F32)
        y = jnp.concatenate([y[:, i * cc:(i + 1) * cc] for i in range(nb)], axis=0)
        z = gates[o] * (y + skip_ref[o:o + 1, :] * z)
    o_ref[...] = z.astype(BF16)


def _hy_conv(proj, g, f3, i3, short_w, short_b, skip, layer, L, row0, nbatch, cc, o_prev):
    nb = L // HY_BLK
    ncc = HY_WIDTH // cc
    rb0 = row0 // L
    ucol = lambda part: pl.BlockSpec((L, cc), lambda c, b: (rb0 + b, (COL_HY + part * HY_WIDTH) // cc + c))
    wcol = lambda part: pl.BlockSpec((None, HY_SHORT, cc), lambda c, b: (layer, 0, part * ncc + c))
    bcol = lambda part: pl.BlockSpec((None, 1, cc), lambda c, b: (layer, 0, part * ncc + c))
    gspec = lambda o: pl.BlockSpec((2 * nb - 1, HY_NFFT, cc), lambda c, b: (0, 0, o * ncc + c))
    in_specs = [
        ucol(0), ucol(1), ucol(2), wcol(0), wcol(1), wcol(2), bcol(0), bcol(1), bcol(2),
        pl.BlockSpec((None, HY_ORDER, cc), lambda c, b: (layer, 0, c)),
        gspec(0), gspec(1),
        pl.BlockSpec((HY_NFFT, 3 * HY_BLK), lambda c, b: (0, 0)),
        pl.BlockSpec((HY_BLK, 3 * HY_NFFT), lambda c, b: (0, 0)),
    ]
    args = [proj, proj, proj, short_w, short_w, short_w] + [short_b.reshape(DEPTH, 1, 3 * HY_WIDTH)] * 3 + [
        skip, g, g, f3, i3]
    aliases = {}
    if o_prev is not None:
        in_specs.append(pl.BlockSpec(memory_space=pl.ANY))
        args.append(o_prev)
        aliases = {len(args) - 1: 0}
    return pl.pallas_call(
        functools.partial(_hy_conv_kernel, nb),
        grid=(ncc, nbatch),
        in_specs=in_specs,
        out_specs=pl.BlockSpec((L, cc), lambda c, b: (rb0 + b, c)),
        out_shape=jax.ShapeDtypeStruct((NT, HY_WIDTH), BF16),
        input_output_aliases=aliases,
        compiler_params=_params(2),
        name=f"hy_conv_{L}",
    )(*args)


_OUT_TM = 512
_OUT_TK = 512


def _layer_norm(x, g, b):
    mu = jnp.mean(x, axis=-1, keepdims=True)
    xc = x - mu
    var = jnp.mean(xc * xc, axis=-1, keepdims=True)
    return xc * lax.rsqrt(var + LN_EPS) * g + b


def _outproj_kernel(oa_ref, ob_ref, oc_ref, w_ref, x_ref, g1_ref, lg_ref, lb_ref, sc_ref, sh_ref,
                    x1_ref, h_ref, acc_ref):
    k = pl.program_id(1)
    w = w_ref[...].astype(BF16)

    @pl.when(k == 0)
    def _():
        acc_ref[...] = jnp.dot(oa_ref[...], w, preferred_element_type=F32)

    @pl.when((k == 1) | (k == 2))
    def _():
        acc_ref[...] += jnp.dot(ob_ref[...], w, preferred_element_type=F32)

    @pl.when(k == 3)
    def _():
        mix = acc_ref[...] + jnp.dot(oc_ref[...], w, preferred_element_type=F32)
        x1 = _layer_norm(DEEPNORM_ALPHA * x_ref[...] + g1_ref[...] * mix, lg_ref[...], lb_ref[...])
        x1_ref[...] = x1
        h_ref[...] = (x1 * (1.0 + sc_ref[...]) + sh_ref[...]).astype(BF16)


def _outproj(oa, ob, oc, w_out, x, mod, ln_g, ln_b, layer):
    tm, tk = _OUT_TM, _OUT_TK
    vec = pl.BlockSpec((None, 1, D_MODEL), lambda i, k: (layer, 0, 0))
    return pl.pallas_call(
        _outproj_kernel,
        grid=(NT // tm, MIX_W // tk),
        in_specs=[
            pl.BlockSpec((tm, tk), lambda i, k: (i, 0)),
            pl.BlockSpec((tm, tk), lambda i, k: (i, jnp.clip(k - 1, 0, 1))),
            pl.BlockSpec((tm, tk), lambda i, k: (i, 0)),
            pl.BlockSpec((None, tk, D_MODEL), lambda i, k: (layer, k, 0)),
            pl.BlockSpec((tm, D_MODEL), lambda i, k: (i, 0)),
            _mod_spec(layer, 2, tm), vec, vec, _mod_spec(layer, 4, tm), _mod_spec(layer, 3, tm),
        ],
        out_specs=[pl.BlockSpec((tm, D_MODEL), lambda i, k: (i, 0)),
                   pl.BlockSpec((tm, D_MODEL), lambda i, k: (i, 0))],
        out_shape=[jax.ShapeDtypeStruct((NT, D_MODEL), F32), jax.ShapeDtypeStruct((NT, D_MODEL), BF16)],
        scratch_shapes=[pltpu.VMEM((tm, D_MODEL), F32)],
        compiler_params=_params(2),
        name="outproj_ln",
    )(oa, ob, oc, w_out, x, mod, ln_g.reshape(DEPTH, 1, D_MODEL), ln_b.reshape(DEPTH, 1, D_MODEL), mod, mod)


_FFN_TM = 1024
_FFN_TF = 256


def _swiglu_partial(h, wg_ref, wu_ref, wd_ref):
    g = jnp.dot(h, wg_ref[...].astype(BF16), preferred_element_type=F32)
    u = jnp.dot(h, wu_ref[...].astype(BF16), preferred_element_type=F32)
    a = (g * jax.nn.sigmoid(g) * u).astype(BF16)
    return jnp.dot(a, wd_ref[...].astype(BF16), preferred_element_type=F32)


def _ffn_kernel(h_ref, wg_ref, wu_ref, wd_ref, y_ref):
    f = pl.program_id(1)
    part = _swiglu_partial(h_ref[...], wg_ref, wu_ref, wd_ref)

    @pl.when(f == 0)
    def _():
        y_ref[...] = part

    @pl.when(f > 0)
    def _():
        y_ref[...] += part


def _ffn_dense(h, wg, wu, wd, idx):
    tm, tf = _FFN_TM, _FFN_TF
    return pl.pallas_call(
        _ffn_kernel,
        grid=(NT // tm, D_FF // tf),
        in_specs=[
            pl.BlockSpec((tm, D_MODEL), lambda i, f: (i, 0)),
            pl.BlockSpec((None, D_MODEL, tf), lambda i, f: (idx, 0, f)),
            pl.BlockSpec((None, D_MODEL, tf), lambda i, f: (idx, 0, f)),
            pl.BlockSpec((None, tf, D_MODEL), lambda i, f: (idx, f, 0)),
        ],
        out_specs=pl.BlockSpec((tm, D_MODEL), lambda i, f: (i, 0)),
        out_shape=jax.ShapeDtypeStruct((NT, D_MODEL), F32),
        compiler_params=_params(2),
        name="ffn_dense",
    )(h, wg, wu, wd)


_RT_TM = 1024


def _router_kernel(h_ref, rt_ref, cw_ref):
    logits = lax.dot_general(rt_ref[...].astype(BF16), h_ref[...], (((1,), (1,)), ((), ())),
                             preferred_element_type=F32)
    idx = lax.broadcasted_iota(jnp.int32, logits.shape, 0)
    m1 = jnp.max(logits, axis=0, keepdims=True)
    i1 = jnp.min(jnp.where(logits == m1, idx, N_EXPERTS), axis=0, keepdims=True)
    rest = jnp.where(idx == i1, -jnp.inf, logits)
    m2 = jnp.max(rest, axis=0, keepdims=True)
    i2 = jnp.min(jnp.where(rest == m2, idx, N_EXPERTS), axis=0, keepdims=True)
    e2 = jnp.exp(m2 - m1)
    g1 = 1.0 / (1.0 + e2)
    g2 = e2 / (1.0 + e2)
    cw_ref[...] = jnp.where(idx == i1, g1, 0.0) + jnp.where(idx == i2, g2, 0.0)


def _router(h, router_t):
    tm = _RT_TM
    return pl.pallas_call(
        _router_kernel,
        grid=(NT // tm,),
        in_specs=[pl.BlockSpec((tm, D_MODEL), lambda i: (i, 0)),
                  pl.BlockSpec((N_EXPERTS, D_MODEL), lambda i: (0, 0))],
        out_specs=pl.BlockSpec((N_EXPERTS, tm), lambda i: (0, i)),
        out_shape=jax.ShapeDtypeStruct((N_EXPERTS, NT), F32),
        compiler_params=_params(1),
        name="moe_router",
    )(h, router_t)


def _moe_dense_kernel(h_ref, cw_ref, wg_ref, wu_ref, wd_ref, y_ref):
    e, f = pl.program_id(1), pl.program_id(2)
    part = cw_ref[...] * _swiglu_partial(h_ref[...], wg_ref, wu_ref, wd_ref)

    @pl.when((e == 0) & (f == 0))
    def _():
        y_ref[...] = part

    @pl.when((e > 0) | (f > 0))
    def _():
        y_ref[...] += part


def _moe(h, router, wg, wu, wd, idx):
    tm, tf = _FFN_TM, _FFN_TF
    cw = _router(h, router[idx].T).reshape(N_EXPERTS, NT, 1)
    return pl.pallas_call(
        _moe_dense_kernel,
        grid=(NT // tm, N_EXPERTS, D_FF // tf),
        in_specs=[
            pl.BlockSpec((tm, D_MODEL), lambda i, e, f: (i, 0)),
            pl.BlockSpec((None, tm, 1), lambda i, e, f: (e, i, 0)),
            pl.BlockSpec((None, None, D_MODEL, tf), lambda i, e, f: (idx, e, 0, f)),
            pl.BlockSpec((None, None, D_MODEL, tf), lambda i, e, f: (idx, e, 0, f)),
            pl.BlockSpec((None, None, tf, D_MODEL), lambda i, e, f: (idx, e, f, 0)),
        ],
        out_specs=pl.BlockSpec((tm, D_MODEL), lambda i, e, f: (i, 0)),
        out_shape=jax.ShapeDtypeStruct((NT, D_MODEL), F32),
        compiler_params=_params(3),
        name="moe_dense",
    )(h, cw, wg, wu, wd)


_EPI_TM = 512


def _epilogue_kernel(x_ref, y_ref, g2_ref, lg_ref, lb_ref, sc_ref, sh_ref, x2_ref, xm_ref):
    x2 = _layer_norm(DEEPNORM_ALPHA * x_ref[...] + g2_ref[...] * y_ref[...], lg_ref[...], lb_ref[...])
    x2_ref[...] = x2
    xm_ref[...] = (x2 * (1.0 + sc_ref[...]) + sh_ref[...]).astype(BF16)


def _epilogue(x1, y, mod, ln_g, ln_b, layer):
    tm = _EPI_TM
    nxt = min(layer + 1, DEPTH - 1)
    vec = pl.BlockSpec((None, 1, D_MODEL), lambda i: (layer, 0, 0))
    row = pl.BlockSpec((tm, D_MODEL), lambda i: (i, 0))
    return pl.pallas_call(
        _epilogue_kernel,
        grid=(NT // tm,),
        in_specs=[row, row, _mod_spec(layer, 5, tm), vec, vec, _mod_spec(nxt, 1, tm), _mod_spec(nxt, 0, tm)],
        out_specs=[row, row],
        out_shape=[jax.ShapeDtypeStruct((NT, D_MODEL), F32), jax.ShapeDtypeStruct((NT, D_MODEL), BF16)],
        compiler_params=_params(1),
        name="ffn_epilogue",
    )(x1, y, mod, ln_g.reshape(DEPTH, 1, D_MODEL), ln_b.reshape(DEPTH, 1, D_MODEL), mod, mod)


def kernel(x_prompt, x_sample, cache_na_k, cache_na_v, cache_swa_k, cache_swa_v, c, c_ctx, w_mod, b_mod, w_in, w_out, na_rpb, swa_sink, hy_short_w, hy_short_b, hy_w1, hy_b1, hy_freq1, hy_w2, hy_b2, hy_freq2, hy_w3, hy_skip, ln1_g, ln1_b, ln2_g, ln2_b, ffn_w_gate, ffn_w_up, ffn_w_down, moe_router, moe_w_gate, moe_w_up, moe_w_down):
    cond = jnp.concatenate([c_ctx[None, :], c], axis=0)
    mod = _modulation(cond, w_mod, b_mod)
    x, xm = _prologue(x_prompt.reshape(NP, D_MODEL), x_sample.reshape(NS, D_MODEL), mod)

    f3_np, i3_np = _dft_mats()
    f3, i3 = jnp.asarray(f3_np), jnp.asarray(i3_np)
    hy_w1p = jnp.pad(hy_w1, ((0, 0), (0, HY_FFN - HY_POS_DIM), (0, 0)))

    kv_out = []
    for l in range(DEPTH):
        proj = _inproj(xm, w_in, l)
        kv_out.append(proj[:NP])

        oa, oc = _ctx_attention(proj, swa_sink, l)
        oa = _na_latent(proj, cache_na_k, cache_na_v, na_rpb[l], oa, l)
        oc = _swa_latent(proj, cache_swa_k, cache_swa_v, swa_sink, oc, l)

        filt = (l, hy_w1p, hy_b1, hy_freq1, hy_w2, hy_b2, hy_freq2, hy_w3)
        g_ctx = _hy_spectrum(*_hy_filter(SEQ, *filt), f3)
        g_lat = _hy_spectrum(*_hy_filter(DEC_SEQ, *filt), f3)
        ob = _hy_conv(proj, g_ctx, f3, i3, hy_short_w, hy_short_b, hy_skip, l, SEQ, 0, BATCH, 512, None)
        ob = _hy_conv(proj, g_lat, f3, i3, hy_short_w, hy_short_b, hy_skip, l, DEC_SEQ, NP, DEC_BATCH, 128, ob)

        x1, h = _outproj(oa, ob, oc, w_out, x, mod, ln1_g, ln1_b, l)
        if l % 2 == 0:
            y = _ffn_dense(h, ffn_w_gate, ffn_w_up, ffn_w_down, l // 2)
        else:
            y = _moe(h, moe_router, moe_w_gate, moe_w_up, moe_w_down, l // 2)
        x, xm = _epilogue(x1, y, mod, ln2_g, ln2_b, l)

    def cache(col, heads):
        parts = [p[:, col:col + heads * HEAD_DIM].reshape(BATCH, SEQ, heads, HEAD_DIM) for p in kv_out]
        return jnp.stack(parts, axis=1)

    return (x[:NP].reshape(BATCH, SEQ, D_MODEL), x[NP:].reshape(DEC_BATCH, DEC_SEQ, D_MODEL),
            cache(COL_KA, NA_HEADS), cache(COL_VA, NA_HEADS),
            cache(COL_KC, SWA_KV_HEADS), cache(COL_VC, SWA_KV_HEADS))
```

```python
import functools
import math

import numpy as np
import jax
import jax.numpy as jnp
from jax import lax
from jax.experimental import pallas as pl
from jax.experimental.pallas import tpu as pltpu

F32 = jnp.float32
BF16 = jnp.bfloat16

D_MODEL = 2048
BATCH = 32
SEQ = 256
DEPTH = 2
DEC_BATCH = 2
DEC_SEQ = 2048
PAST_LEN = 256
GRID_W = 64
GRID_ROWS = DEC_SEQ // GRID_W
HEAD_DIM = 64
ATTN_SCALE = HEAD_DIM ** -0.5
NA_HEADS = 8
NA_WIN_ROWS = 8
NA_WIN_COLS = 16
HY_WIDTH = 1024
HY_ORDER = 2
HY_SHORT = 3
HY_POS_BANDS = 16
HY_POS_DIM = 1 + 2 * HY_POS_BANDS
HY_FFN = 64
HY_FAST_DECAY = 0.3
HY_SLOW_DECAY = 1.5
HY_DECAY_TARGET = 1e-2
HY_MOD_SHIFT = 0.05
SWA_HEADS = 8
SWA_KV_HEADS = 2
SWA_GROUP = SWA_HEADS // SWA_KV_HEADS
SWA_WINDOW = 128
SWA_BLOCK = 128
NA_W = NA_HEADS * HEAD_DIM
SWA_W = SWA_HEADS * HEAD_DIM
SWA_KV_W = SWA_KV_HEADS * HEAD_DIM
MIX_W = NA_W + HY_WIDTH + SWA_W
IN_W = 3 * NA_W + 3 * HY_WIDTH + SWA_W + 2 * SWA_KV_W
D_FF = 7168
N_EXPERTS = 8
TOP_K = 2
D_FF_EXPERT = 7168
ROPE_THETA = 10000.0
LN_EPS = 1e-5
NEG_INF = -1e30
DEEPNORM_ALPHA = (2 * DEPTH) ** 0.25

NP = BATCH * SEQ
NS = DEC_BATCH * DEC_SEQ
NT = NP + NS
N_COND = 1 + DEC_BATCH
COND_PAD = 8

COL_QA, COL_KA, COL_VA = 0, NA_W, 2 * NA_W
COL_HY = 3 * NA_W
COL_QC = COL_HY + 3 * HY_WIDTH
COL_KC = COL_QC + SWA_W
COL_VC = COL_KC + SWA_KV_W

HY_BLK = 256
HY_NFFT = 2 * HY_BLK

LANES = 128
V7X_VMEM_BYTES = 64 * 1024 * 1024
VMEM_LIMIT_BYTES = V7X_VMEM_BYTES * 7 // 8


def _params(grid_rank):
    return pltpu.CompilerParams(dimension_semantics=("arbitrary",) * grid_rank,
                                vmem_limit_bytes=VMEM_LIMIT_BYTES)


def _cond_of_row(row):
    return jnp.maximum((row - NP) // DEC_SEQ + 1, 0)


def _bf16_split_np(a):
    a = np.asarray(a, np.float32)
    hi = a.astype(BF16)
    lo = (a - hi.astype(np.float32)).astype(BF16)
    return hi, lo


_MOD_TN = 512


def _mod_kernel(cb_ref, w_ref, b_ref, o_ref, s_scr):
    @pl.when((pl.program_id(0) == 0) & (pl.program_id(1) == 0))
    def _():
        c = cb_ref[...]
        s_scr[...] = c * jax.nn.sigmoid(c)

    o_ref[...] = jnp.zeros_like(o_ref)
    for n in range(_MOD_TN // LANES):
        sl = slice(n * LANES, (n + 1) * LANES)
        w = w_ref[:, sl]
        for j in range(N_COND):
            acc = jnp.sum(w * s_scr[j], axis=0, keepdims=True)
            o_ref[j:j + 1, sl] = acc + b_ref[:, sl]


def _modulation(cond, w_mod, b_mod):
    cb = jnp.broadcast_to(cond[:, :, None], (N_COND, D_MODEL, LANES))
    n6 = 6 * D_MODEL
    out = pl.pallas_call(
        _mod_kernel,
        grid=(DEPTH, n6 // _MOD_TN),
        in_specs=[
            pl.BlockSpec((N_COND, D_MODEL, LANES), lambda l, j: (0, 0, 0)),
            pl.BlockSpec((None, D_MODEL, _MOD_TN), lambda l, j: (l, 0, j)),
            pl.BlockSpec((None, 1, _MOD_TN), lambda l, j: (l, 0, j)),
        ],
        out_specs=pl.BlockSpec((None, COND_PAD, _MOD_TN), lambda l, j: (l, 0, j)),
        out_shape=jax.ShapeDtypeStruct((DEPTH, COND_PAD, n6), F32),
        scratch_shapes=[pltpu.VMEM((N_COND, D_MODEL, LANES), F32)],
        compiler_params=_params(2),
        name="adaln_mod",
    )(cb, w_mod, b_mod.reshape(DEPTH, 1, n6))
    out = out.reshape(DEPTH, COND_PAD, 6, D_MODEL).transpose(0, 2, 1, 3)
    return out.reshape(DEPTH, 6, COND_PAD, 1, D_MODEL)


def _mod_spec(layer, which, tm):
    return pl.BlockSpec((None, None, None, 1, D_MODEL),
                        lambda i, *_: (layer, which, _cond_of_row(i * tm), 0, 0))


_PRO_TM = 512


def _prologue_kernel(xp_ref, xs_ref, sc_ref, sh_ref, x_ref, xm_ref):
    i = pl.program_id(0)

    def emit(x):
        x_ref[...] = x
        xm_ref[...] = (x * (1.0 + sc_ref[...]) + sh_ref[...]).astype(BF16)

    @pl.when(i < NP // _PRO_TM)
    def _():
        emit(xp_ref[...])

    @pl.when(i >= NP // _PRO_TM)
    def _():
        emit(xs_ref[...])


def _prologue(xp, xs, mod):
    tm = _PRO_TM
    npt = NP // tm
    return pl.pallas_call(
        _prologue_kernel,
        grid=(NT // tm,),
        in_specs=[
            pl.BlockSpec((tm, D_MODEL), lambda i: (jnp.minimum(i, npt - 1), 0)),
            pl.BlockSpec((tm, D_MODEL), lambda i: (jnp.maximum(i - npt, 0), 0)),
            _mod_spec(0, 1, tm),
            _mod_spec(0, 0, tm),
        ],
        out_specs=[pl.BlockSpec((tm, D_MODEL), lambda i: (i, 0)),
                   pl.BlockSpec((tm, D_MODEL), lambda i: (i, 0))],
        out_shape=[jax.ShapeDtypeStruct((NT, D_MODEL), F32),
                   jax.ShapeDtypeStruct((NT, D_MODEL), BF16)],
        compiler_params=_params(1),
        name="prologue",
    )(xp, xs, mod, mod)


_INP_TM = 1024
_INP_TN = 768


def _inproj_kernel(x_ref, w_ref, o_ref):
    o_ref[...] = jnp.dot(x_ref[...], w_ref[...].astype(BF16), preferred_element_type=F32)


def _inproj(xm, w_in, layer):
    tm, tn = _INP_TM, _INP_TN
    return pl.pallas_call(
        _inproj_kernel,
        grid=(NT // tm, IN_W // tn),
        in_specs=[
            pl.BlockSpec((tm, D_MODEL), lambda i, j: (i, 0)),
            pl.BlockSpec((None, D_MODEL, tn), lambda i, j: (layer, 0, j)),
        ],
        out_specs=pl.BlockSpec((tm, tn), lambda i, j: (i, j)),
        out_shape=jax.ShapeDtypeStruct((NT, IN_W), F32),
        compiler_params=_params(2),
        name="inproj",
    )(xm, w_in)


def _attend(q, k, v, bias=None, valid=None, sink=None):
    s = lax.dot_general(q, k, (((1,), (1,)), ((), ())), preferred_element_type=F32)
    if bias is not None:
        s = s + bias
    if valid is not None:
        s = jnp.where(valid, s, NEG_INF)
    m = jnp.max(s, axis=-1, keepdims=True)
    if sink is not None:
        m = jnp.maximum(m, sink)
    p = jnp.exp(s - m)
    denom = jnp.sum(p, axis=-1, keepdims=True)
    if sink is not None:
        denom = denom + jnp.exp(sink - m)
    o = jnp.dot(p.astype(BF16), v, preferred_element_type=F32)
    return o / denom


def _head(x, h):
    return x[:, h * HEAD_DIM:(h + 1) * HEAD_DIM]


def _rope(x, cos, sin_signed):
    width = x.shape[1]
    lane = lax.broadcasted_iota(jnp.int32, x.shape, 1)
    first = (lane % 32) < 16
    partner = jnp.where(first, pltpu.roll(x, width - 16, 1), pltpu.roll(x, 16, 1))
    return x * cos + partner * sin_signed


def _rope_tables(n_heads):
    half = HEAD_DIM // 2
    inv = ROPE_THETA ** (-np.arange(0, half, 2, dtype=np.float64) / half)
    t = np.arange(DEC_SEQ)
    inv = inv.astype(np.float32).astype(np.float64)
    ang_r = (t // GRID_W)[:, None].astype(np.float64) * inv[None, :]
    ang_c = (t % GRID_W)[:, None].astype(np.float64) * inv[None, :]
    cos = np.concatenate([np.cos(ang_r), np.cos(ang_r), np.cos(ang_c), np.cos(ang_c)], -1)
    sin = np.concatenate([-np.sin(ang_r), np.sin(ang_r), -np.sin(ang_c), np.sin(ang_c)], -1)
    cos = np.tile(cos, (1, n_heads)).astype(np.float32)
    sin = np.tile(sin, (1, n_heads)).astype(np.float32)
    return cos, sin


def _ctx_attn_kernel(layer, sink_ref, qa_ref, ka_ref, va_ref, qc_ref, kvc_ref, oa_ref, oc_ref):
    qa = (qa_ref[...] * ATTN_SCALE).astype(BF16)
    ka = ka_ref[...].astype(BF16)
    va = va_ref[...].astype(BF16)
    outs = [_attend(_head(qa, h), _head(ka, h), _head(va, h)) for h in range(NA_HEADS)]
    oa_ref[...] = jnp.concatenate(outs, axis=-1).astype(BF16)

    qc = (qc_ref[...] * ATTN_SCALE).astype(BF16)
    kvc = kvc_ref[...].astype(BF16)
    kc, vc = kvc[:, :SWA_KV_W], kvc[:, SWA_KV_W:]
    outs = []
    for h in range(SWA_HEADS):
        hk = h // SWA_GROUP
        outs.append(_attend(_head(qc, h), _head(kc, hk), _head(vc, hk), sink=sink_ref[layer, h]))
    oc_ref[...] = jnp.concatenate(outs, axis=-1).astype(BF16)


def _ctx_attention(proj, swa_sink, layer):
    blk = lambda w, c: pl.BlockSpec((SEQ, w), lambda b: (b, c // w))
    return pl.pallas_call(
        functools.partial(_ctx_attn_kernel, layer),
        grid=(BATCH,),
        in_specs=[
            pl.BlockSpec(memory_space=pltpu.SMEM),
            blk(NA_W, COL_QA), blk(NA_W, COL_KA), blk(NA_W, COL_VA),
            blk(SWA_W, COL_QC), blk(2 * SWA_KV_W, COL_KC),
        ],
        out_specs=[pl.BlockSpec((SEQ, NA_W), lambda b: (b, 0)),
                   pl.BlockSpec((SEQ, SWA_W), lambda b: (b, 0))],
        out_shape=[jax.ShapeDtypeStruct((NP, NA_W), BF16),
                   jax.ShapeDtypeStruct((NP, SWA_W), BF16)],
        compiler_params=_params(1),
        name="ctx_attn",
    )(swa_sink, proj, proj, proj, proj, proj)


_NA_BAND = NA_WIN_ROWS * GRID_W


def _na_band_start(r):
    return jnp.clip(r - NA_WIN_ROWS // 2, 0, GRID_ROWS - NA_WIN_ROWS)


def _na_kernel(q_ref, k_ref, v_ref, ck_ref, cv_ref, bias_ref, o_ref):
    r = pl.program_id(1)
    start = pl.multiple_of(_na_band_start(r) * GRID_W, GRID_W)
    q = (q_ref[...] * ATTN_SCALE).astype(BF16)
    k = jnp.concatenate([k_ref[pl.ds(start, _NA_BAND), :].astype(BF16), ck_ref[...].astype(BF16)], axis=0)
    v = jnp.concatenate([v_ref[pl.ds(start, _NA_BAND), :].astype(BF16), cv_ref[...].astype(BF16)], axis=0)
    outs = [_attend(_head(q, h), _head(k, h), _head(v, h), bias=bias_ref[h]) for h in range(NA_HEADS)]
    o_ref[...] = jnp.concatenate(outs, axis=-1).astype(BF16)


def _na_bias_tables(rpb_l):
    nc = 2 * NA_WIN_COLS - 1
    col = np.arange(GRID_W)
    dc = np.clip(col[None, :] - col[:, None], -(NA_WIN_COLS - 1), NA_WIN_COLS - 1) + NA_WIN_COLS - 1
    onehot = (dc.reshape(-1)[None, :] == np.arange(nc)[:, None]).astype(np.float32)
    cs = np.clip(col - NA_WIN_COLS // 2, 0, GRID_W - NA_WIN_COLS)
    in_win = (col[None, :] >= cs[:, None]) & (col[None, :] < cs[:, None] + NA_WIN_COLS)
    toe = jnp.einsum('hrd,dn->hrn', rpb_l.astype(F32), jnp.asarray(onehot),
                     precision=lax.Precision.HIGHEST)
    toe = toe.reshape(NA_HEADS, 2 * NA_WIN_ROWS - 1, GRID_W, GRID_W)
    toe = jnp.where(jnp.asarray(in_win)[None, None], toe, NEG_INF)
    strips = jnp.stack([toe[:, o:o + NA_WIN_ROWS] for o in range(NA_WIN_ROWS)], 0)
    strips = strips.transpose(0, 1, 3, 2, 4).reshape(NA_WIN_ROWS, NA_HEADS, GRID_W, _NA_BAND)
    return jnp.pad(strips, ((0, 0), (0, 0), (0, 0), (0, PAST_LEN)))


def _na_latent(proj, cache_k, cache_v, rpb_l, layer):
    bias = _na_bias_tables(rpb_l)
    row0 = NP // GRID_W
    qmap = lambda b, r: (row0 + b * GRID_ROWS + r, 0)
    kvspec = lambda c: pl.BlockSpec((DEC_SEQ, NA_W), lambda b, r: (NP // DEC_SEQ + b, c // NA_W))
    cspec = pl.BlockSpec((None, None, PAST_LEN, NA_W), lambda b, r: (b, layer, 0, 0))

    def bias_map(b, r):
        return (_na_band_start(r) - r + NA_WIN_ROWS - 1, 0, 0, 0)

    return pl.pallas_call(
        _na_kernel,
        grid=(DEC_BATCH, GRID_ROWS),
        in_specs=[
            pl.BlockSpec((GRID_W, NA_W), qmap),
            kvspec(COL_KA), kvspec(COL_VA), cspec, cspec,
            pl.BlockSpec((None, NA_HEADS, GRID_W, _NA_BAND + PAST_LEN), bias_map),
        ],
        out_specs=pl.BlockSpec((GRID_W, NA_W), lambda b, r: (b * GRID_ROWS + r, 0)),
        out_shape=jax.ShapeDtypeStruct((NS, NA_W), BF16),
        compiler_params=_params(2),
        name="na_latent",
    )(proj, proj, proj, cache_k.reshape(DEC_BATCH, DEPTH, PAST_LEN, NA_W),
      cache_v.reshape(DEC_BATCH, DEPTH, PAST_LEN, NA_W), bias)


_SWA_KEYS = 3 * SWA_BLOCK


def _swa_kernel(layer, sink_ref, q_ref, kv_ref, ck_ref, cv_ref, cq_ref, sq_ref, ckt_ref, skt_ref,
                o_ref):
    n = pl.program_id(1)
    start = pl.multiple_of(jnp.clip((n - 1) * SWA_BLOCK, 0, DEC_SEQ - _SWA_KEYS), SWA_BLOCK)
    q = (_rope(q_ref[...], cq_ref[...], sq_ref[...]) * ATTN_SCALE).astype(BF16)
    kv = kv_ref[pl.ds(start, _SWA_KEYS), :]
    kw = _rope(kv[:, :SWA_KV_W], ckt_ref[pl.ds(start, _SWA_KEYS), :], skt_ref[pl.ds(start, _SWA_KEYS), :])
    k = jnp.concatenate([kw.astype(BF16), ck_ref[...].astype(BF16)], axis=0)
    v = jnp.concatenate([kv[:, SWA_KV_W:].astype(BF16), cv_ref[...].astype(BF16)], axis=0)
    nk = _SWA_KEYS + PAST_LEN
    qpos = n * SWA_BLOCK + lax.broadcasted_iota(jnp.int32, (SWA_BLOCK, nk), 0)
    kidx = lax.broadcasted_iota(jnp.int32, (SWA_BLOCK, nk), 1)
    valid = (jnp.abs(qpos - (start + kidx)) <= SWA_WINDOW) | (kidx >= _SWA_KEYS)
    outs = []
    for h in range(SWA_HEADS):
        hk = h // SWA_GROUP
        outs.append(_attend(_head(q, h), _head(k, hk), _head(v, hk), valid=valid, sink=sink_ref[layer, h]))
    o_ref[...] = jnp.concatenate(outs, axis=-1).astype(BF16)


def _swa_latent(proj, cache_k, cache_v, swa_sink, layer):
    cq, sq = _rope_tables(SWA_HEADS)
    ck, sk = _rope_tables(SWA_KV_HEADS)
    row0 = NP // SWA_BLOCK
    nblk = DEC_SEQ // SWA_BLOCK
    qmap = lambda b, n: (row0 + b * nblk + n, COL_QC // SWA_W)
    omap = lambda b, n: (b * nblk + n, 0)
    cspec = pl.BlockSpec((None, None, PAST_LEN, SWA_KV_W), lambda b, n: (b, layer, 0, 0))
    tq = pl.BlockSpec((SWA_BLOCK, SWA_W), lambda b, n: (n, 0))
    tk = pl.BlockSpec((DEC_SEQ, SWA_KV_W), lambda b, n: (0, 0))
    return pl.pallas_call(
        functools.partial(_swa_kernel, layer),
        grid=(DEC_BATCH, nblk),
        in_specs=[
            pl.BlockSpec(memory_space=pltpu.SMEM),
            pl.BlockSpec((SWA_BLOCK, SWA_W), qmap),
            pl.BlockSpec((DEC_SEQ, 2 * SWA_KV_W), lambda b, n: (NP // DEC_SEQ + b, COL_KC // (2 * SWA_KV_W))),
            cspec, cspec, tq, tq, tk, tk,
        ],
        out_specs=pl.BlockSpec((SWA_BLOCK, SWA_W), omap),
        out_shape=jax.ShapeDtypeStruct((NS, SWA_W), BF16),
        compiler_params=_params(2),
        name="swa_latent",
    )(swa_sink, proj, proj, cache_k.reshape(DEC_BATCH, DEPTH, PAST_LEN, SWA_KV_W),
      cache_v.reshape(DEC_BATCH, DEPTH, PAST_LEN, SWA_KV_W),
      jnp.asarray(cq), jnp.asarray(sq), jnp.asarray(ck), jnp.asarray(sk))


def _dft_mats():
    bk, n = HY_BLK, HY_NFFT
    s = np.arange(bk)
    ang = 2.0 * np.pi * ((s[:, None] * s[None, :]) % n) / n
    fwd = np.zeros((n, bk))
    fwd[:bk] = np.cos(ang)
    fwd[bk:] = -np.sin(ang)
    fwd[bk] = (-1.0) ** s
    inv = np.zeros((bk, n))
    inv[:, :bk] = (2.0 / n) * np.cos(ang)
    inv[:, 0] = 1.0 / n
    inv[:, bk:] = -(2.0 / n) * np.sin(ang)
    inv[:, bk] = (1.0 / n) * (-1.0) ** s
    fh, fl = _bf16_split_np(fwd)
    ih, il = _bf16_split_np(inv)
    return np.concatenate([fh, fh, fl], axis=1), np.concatenate([ih, ih, il], axis=1)


def _split3(x):
    hi = x.astype(BF16)
    lo = (x - hi.astype(F32)).astype(BF16)
    return jnp.concatenate([hi, lo, hi], axis=0)


def _hy_features(L):
    pos = np.arange(L, dtype=np.float64)
    t = (pos.astype(np.float32) / np.float32(max(L - 1, 1))).astype(np.float64)
    bands = np.linspace(1e-4, HY_POS_BANDS - 1, HY_POS_BANDS, dtype=np.float32).astype(np.float64)
    ang = np.float64(np.float32(2.0 * math.pi / L)) * pos[:, None] * bands[None, :]
    feat = np.concatenate([t[:, None], np.cos(ang), -np.sin(ang)], -1)
    feat = np.pad(feat, ((0, 0), (0, HY_FFN - HY_POS_DIM)))
    deltas = np.abs(np.linspace(math.log(HY_DECAY_TARGET) / HY_SLOW_DECAY,
                                math.log(HY_DECAY_TARGET) / HY_FAST_DECAY, HY_WIDTH, dtype=np.float32))
    window = np.exp(-t[:, None] * deltas[None, :].astype(np.float64)) + HY_MOD_SHIFT
    return feat.astype(np.float32), window.astype(np.float32)


_HYF_TC = 512


def _hy_filter_kernel(feat_ref, w1_ref, b1_ref, f1_ref, w2_ref, b2_ref, f2_ref, w3f_ref, w3b_ref, win_ref,
                      hf_ref, hb_ref):
    hp = lax.Precision.HIGHEST
    z = jnp.dot(feat_ref[...], w1_ref[...], precision=hp, preferred_element_type=F32) + b1_ref[...]
    h = jnp.sin(f1_ref[...] * z)
    z = jnp.dot(h, w2_ref[...], precision=hp, preferred_element_type=F32) + b2_ref[...]
    h = jnp.sin(f2_ref[...] * z)
    win = win_ref[...]
    af = jnp.dot(h, w3f_ref[...], precision=hp, preferred_element_type=F32) * win
    ab = jnp.dot(h, w3b_ref[...], precision=hp, preferred_element_type=F32) * win
    nrm = jnp.sum(jnp.abs(af), axis=0, keepdims=True) + jnp.sum(jnp.abs(ab), axis=0, keepdims=True)
    hf_ref[...] = af / nrm
    hb_ref[...] = ab / nrm


def _hy_filter(L, layer, hy_w1p, hy_b1, hy_freq1, hy_w2, hy_b2, hy_freq2, hy_w3):
    feat, window = _hy_features(L)
    tc = _HYF_TC
    ncc = HY_WIDTH // tc
    ow = HY_ORDER * HY_WIDTH
    small = lambda shape: pl.BlockSpec((None,) + shape, lambda o, c: (layer,) + (0,) * len(shape))
    w3spec = lambda d: pl.BlockSpec((None, HY_FFN, tc), lambda o, c: (layer, 0, (d * HY_ORDER + o) * ncc + c))
    ospec = pl.BlockSpec((L, tc), lambda o, c: (0, o * ncc + c))
    vec = lambda a: a.reshape(DEPTH, 1, HY_FFN)
    return pl.pallas_call(
        _hy_filter_kernel,
        grid=(HY_ORDER, ncc),
        in_specs=[
            pl.BlockSpec((L, HY_FFN), lambda o, c: (0, 0)),
            small((HY_FFN, HY_FFN)), small((1, HY_FFN)), small((1, HY_FFN)),
            small((HY_FFN, HY_FFN)), small((1, HY_FFN)), small((1, HY_FFN)),
            w3spec(0), w3spec(1),
            pl.BlockSpec((L, tc), lambda o, c: (0, c)),
        ],
        out_specs=[ospec, ospec],
        out_shape=[jax.ShapeDtypeStruct((L, ow), F32), jax.ShapeDtypeStruct((L, ow), F32)],
        compiler_params=_params(2),
        name=f"hy_filter_{L}",
    )(jnp.asarray(feat), hy_w1p, vec(hy_b1), vec(hy_freq1), hy_w2, vec(hy_b2), vec(hy_freq2),
      hy_w3, hy_w3, jnp.asarray(window))


_HYS_TC = 256


def _hy_spectrum_kernel(nb, hf_ref, hb_ref, f3_ref, g_ref):
    n, bk = HY_NFFT, HY_BLK
    row = lax.broadcasted_iota(jnp.int32, (n, 1), 0)
    ones_lo = (row <= bk).astype(F32)
    sgn = jnp.where(row % 2 == 0, 1.0, -1.0).astype(F32)
    conj = jnp.where(row > bk, -1.0, 1.0).astype(F32)
    f3 = f3_ref[...]

    def spectra(h_ref, drop_lag0):
        first, tail = [], []
        for b in range(nb):
            blk = h_ref[b * bk:(b + 1) * bk, :]
            t = jnp.dot(f3, _split3(blk), preferred_element_type=F32)
            head = ones_lo * blk[0:1, :]
            if b == 0 and drop_lag0:
                t = t - head
                tail.append(sgn * t)
            else:
                tail.append(sgn * (t - head))
            first.append(t)
        return [first[d] + (tail[d - 1] if d >= 1 else 0.0) for d in range(nb)]

    gf = spectra(hf_ref, False)
    gb = spectra(hb_ref, True)
    g_ref[nb - 1] = gf[0] + conj * gb[0]
    for d in range(1, nb):
        g_ref[nb - 1 + d] = gf[d]
        g_ref[nb - 1 - d] = conj * gb[d]


def _hy_spectrum(hf, hb, f3):
    L = hf.shape[0]
    nb = L // HY_BLK
    ow = HY_ORDER * HY_WIDTH
    tc = _HYS_TC
    return pl.pallas_call(
        functools.partial(_hy_spectrum_kernel, nb),
        grid=(ow // tc,),
        in_specs=[
            pl.BlockSpec((L, tc), lambda c: (0, c)),
            pl.BlockSpec((L, tc), lambda c: (0, c)),
            pl.BlockSpec((HY_NFFT, 3 * HY_BLK), lambda c: (0, 0)),
        ],
        out_specs=pl.BlockSpec((2 * nb - 1, HY_NFFT, tc), lambda c: (0, 0, c)),
        out_shape=jax.ShapeDtypeStruct((2 * nb - 1, HY_NFFT, ow), F32),
        compiler_params=_params(1),
        name=f"hy_spectrum_{L}",
    )(hf, hb, f3)


def _spec_mul(x, g):
    bk = HY_BLK
    xa, xb = x[:bk], x[bk:]
    ga, gb = g[:bk], g[bk:]
    row0 = lax.broadcasted_iota(jnp.int32, xa.shape, 0) == 0
    bb = xb * gb
    pa = xa * ga - jnp.where(row0, 0.0, bb)
    pb = jnp.where(row0, bb, xa * gb + xb * ga)
    return jnp.concatenate([pa, pb], axis=0)


def _short_conv(u, w, b):
    L = u.shape[0]
    row = lax.broadcasted_iota(jnp.int32, u.shape, 0)
    prev = jnp.where(row == 0, 0.0, pltpu.roll(u, 1, 0))
    nxt = jnp.where(row == L - 1, 0.0, pltpu.roll(u, L - 1, 0))
    return prev * w[0:1] + u * w[1:2] + nxt * w[2:3] + b


def _hy_conv_kernel(nb, uv_ref, u1_ref, u2_ref, wv_ref, w1_ref, w2_ref, bv_ref, b1_ref, b2_ref, skip_ref,
                    g0_ref, g1_ref, f3_ref, i3_ref, o_ref):
    bk = HY_BLK
    v = _short_conv(uv_ref[...], wv_ref[...], bv_ref[...])
    gates = (_short_conv(u1_ref[...], w1_ref[...], b1_ref[...]),
             _short_conv(u2_ref[...], w2_ref[...], b2_ref[...]))
    cc = v.shape[1]
    f3 = f3_ref[...]
    i3 = i3_ref[...]
    z = v
    for o, g_ref in enumerate((g0_ref, g1_ref)):
        zcat = jnp.concatenate([z[j * bk:(j + 1) * bk] for j in range(nb)], axis=1)
        x = jnp.dot(f3, _split3(zcat), preferred_element_type=F32)
        prods = []
        for i in range(nb):
            acc = None
            for j in range(nb):
                t = _spec_mul(x[:, j * cc:(j + 1) * cc], g_ref[i - j + nb - 1])
                acc = t if acc is None else acc + t
            prods.append(acc)
        pcat = jnp.concatenate(prods, axis=1)
        y = jnp.dot(i3, _split3(pcat), preferred_element_type=F32)
        y = jnp.concatenate([y[:, i * cc:(i + 1) * cc] for i in range(nb)], axis=0)
        z = gates[o] * (y + skip_ref[o:o + 1, :] * z)
    o_ref[...] = z.astype(BF16)


def _hy_conv(proj, g, f3, i3, short_w, short_b, skip, layer, L, row0, nbatch, cc):
    nb = L // HY_BLK
    ncc = HY_WIDTH // cc
    rb0 = row0 // L
    ucol = lambda part: pl.BlockSpec((L, cc), lambda c, b: (rb0 + b, (COL_HY + part * HY_WIDTH) // cc + c))
    wcol = lambda part: pl.BlockSpec((None, HY_SHORT, cc), lambda c, b: (layer, 0, part * ncc + c))
    bcol = lambda part: pl.BlockSpec((None, 1, cc), lambda c, b: (layer, 0, part * ncc + c))
    gspec = lambda o: pl.BlockSpec((2 * nb - 1, HY_NFFT, cc), lambda c, b: (0, 0, o * ncc + c))
    in_specs = [
        ucol(0), ucol(1), ucol(2), wcol(0), wcol(1), wcol(2), bcol(0), bcol(1), bcol(2),
        pl.BlockSpec((None, HY_ORDER, cc), lambda c, b: (layer, 0, c)),
        gspec(0), gspec(1),
        pl.BlockSpec((HY_NFFT, 3 * HY_BLK), lambda c, b: (0, 0)),
        pl.BlockSpec((HY_BLK, 3 * HY_NFFT), lambda c, b: (0, 0)),
    ]
    args = [proj, proj, proj, short_w, short_w, short_w] + [short_b.reshape(DEPTH, 1, 3 * HY_WIDTH)] * 3 + [
        skip, g, g, f3, i3]
    return pl.pallas_call(
        functools.partial(_hy_conv_kernel, nb),
        grid=(ncc, nbatch),
        in_specs=in_specs,
        out_specs=pl.BlockSpec((L, cc), lambda c, b: (b, c)),
        out_shape=jax.ShapeDtypeStruct((nbatch * L, HY_WIDTH), BF16),
        compiler_params=_params(2),
        name=f"hy_conv_{L}",
    )(*args)


_OUT_TM = 512
_OUT_TK = 512


def _layer_norm(x, g, b):
    mu = jnp.mean(x, axis=-1, keepdims=True)
    xc = x - mu
    var = jnp.mean(xc * xc, axis=-1, keepdims=True)
    return xc * lax.rsqrt(var + LN_EPS) * g + b


def _outproj_kernel(oa_c, ob_c, oc_c, oa_l, ob_l, oc_l, w_ref, x_ref, g1_ref, lg_ref, lb_ref, sc_ref, sh_ref,
                    x1_ref, h_ref, acc_ref):
    i, k = pl.program_id(0), pl.program_id(1)
    w = w_ref[...].astype(BF16)
    is_ctx = i < NP // _OUT_TM

    def mixer(c_ref, l_ref):
        return jnp.where(is_ctx, c_ref[...], l_ref[...])

    @pl.when(k == 0)
    def _():
        acc_ref[...] = jnp.dot(mixer(oa_c, oa_l), w, preferred_element_type=F32)

    @pl.when((k == 1) | (k == 2))
    def _():
        acc_ref[...] += jnp.dot(mixer(ob_c, ob_l), w, preferred_element_type=F32)

    @pl.when(k == 3)
    def _():
        mix = acc_ref[...] + jnp.dot(mixer(oc_c, oc_l), w, preferred_element_type=F32)
        x1 = _layer_norm(DEEPNORM_ALPHA * x_ref[...] + g1_ref[...] * mix, lg_ref[...], lb_ref[...])
        x1_ref[...] = x1
        h_ref[...] = (x1 * (1.0 + sc_ref[...]) + sh_ref[...]).astype(h_ref.dtype)


def _outproj(ctx_mix, lat_mix, w_out, x, mod, ln_g, ln_b, layer, h_dtype):
    tm, tk = _OUT_TM, _OUT_TK
    npt = NP // tm
    vec = pl.BlockSpec((None, 1, D_MODEL), lambda i, k: (layer, 0, 0))
    crow = lambda i: jnp.minimum(i, npt - 1)
    lrow = lambda i: jnp.maximum(i - npt, 0)
    kb = lambda k: jnp.clip(k - 1, 0, 1)
    return pl.pallas_call(
        _outproj_kernel,
        grid=(NT // tm, MIX_W // tk),
        in_specs=[
            pl.BlockSpec((tm, tk), lambda i, k: (crow(i), 0)),
            pl.BlockSpec((tm, tk), lambda i, k: (crow(i), kb(k))),
            pl.BlockSpec((tm, tk), lambda i, k: (crow(i), 0)),
            pl.BlockSpec((tm, tk), lambda i, k: (lrow(i), 0)),
            pl.BlockSpec((tm, tk), lambda i, k: (lrow(i), kb(k))),
            pl.BlockSpec((tm, tk), lambda i, k: (lrow(i), 0)),
            pl.BlockSpec((None, tk, D_MODEL), lambda i, k: (layer, k, 0)),
            pl.BlockSpec((tm, D_MODEL), lambda i, k: (i, 0)),
            _mod_spec(layer, 2, tm), vec, vec, _mod_spec(layer, 4, tm), _mod_spec(layer, 3, tm),
        ],
        out_specs=[pl.BlockSpec((tm, D_MODEL), lambda i, k: (i, 0)),
                   pl.BlockSpec((tm, D_MODEL), lambda i, k: (i, 0))],
        out_shape=[jax.ShapeDtypeStruct((NT, D_MODEL), F32), jax.ShapeDtypeStruct((NT, D_MODEL), h_dtype)],
        scratch_shapes=[pltpu.VMEM((tm, D_MODEL), F32)],
        compiler_params=_params(2),
        name="outproj_ln",
    )(*ctx_mix, *lat_mix, w_out, x, mod, ln_g.reshape(DEPTH, 1, D_MODEL), ln_b.reshape(DEPTH, 1, D_MODEL),
      mod, mod)


_FFN_TM = 1024
_FFN_TF = 256


def _swiglu_partial(h, wg_ref, wu_ref, wd_ref):
    g = jnp.dot(h, wg_ref[...].astype(BF16), preferred_element_type=F32)
    u = jnp.dot(h, wu_ref[...].astype(BF16), preferred_element_type=F32)
    a = (g * jax.nn.sigmoid(g) * u).astype(BF16)
    return jnp.dot(a, wd_ref[...].astype(BF16), preferred_element_type=F32)


def _ffn_kernel(h_ref, wg_ref, wu_ref, wd_ref, y_ref):
    f = pl.program_id(1)
    part = _swiglu_partial(h_ref[...], wg_ref, wu_ref, wd_ref)

    @pl.when(f == 0)
    def _():
        y_ref[...] = part

    @pl.when(f > 0)
    def _():
        y_ref[...] += part


def _ffn_dense(h, wg, wu, wd, idx):
    tm, tf = _FFN_TM, _FFN_TF
    return pl.pallas_call(
        _ffn_kernel,
        grid=(NT // tm, D_FF // tf),
        in_specs=[
            pl.BlockSpec((tm, D_MODEL), lambda i, f: (i, 0)),
            pl.BlockSpec((None, D_MODEL, tf), lambda i, f: (idx, 0, f)),
            pl.BlockSpec((None, D_MODEL, tf), lambda i, f: (idx, 0, f)),
            pl.BlockSpec((None, tf, D_MODEL), lambda i, f: (idx, f, 0)),
        ],
        out_specs=pl.BlockSpec((tm, D_MODEL), lambda i, f: (i, 0)),
        out_shape=jax.ShapeDtypeStruct((NT, D_MODEL), F32),
        compiler_params=_params(2),
        name="ffn_dense",
    )(h, wg, wu, wd)


_RT_TM = 1024


def _router_kernel(h_ref, rt_ref, r_ref):
    logits = lax.dot_general(rt_ref[...].astype(BF16), h_ref[...].astype(BF16), (((1,), (1,)), ((), ())),
                             preferred_element_type=F32)
    idx = lax.broadcasted_iota(jnp.int32, logits.shape, 0)
    m1 = jnp.max(logits, axis=0, keepdims=True)
    i1 = jnp.min(jnp.where(logits == m1, idx, N_EXPERTS), axis=0, keepdims=True)
    rest = jnp.where(idx == i1, -jnp.inf, logits)
    m2 = jnp.max(rest, axis=0, keepdims=True)
    i2 = jnp.min(jnp.where(rest == m2, idx, N_EXPERTS), axis=0, keepdims=True)
    e2 = jnp.exp(m2 - m1)
    g1 = 1.0 / (1.0 + e2)
    g2 = e2 / (1.0 + e2)
    r_ref[...] = (jnp.where(idx == 0, i1.astype(F32), 0.0) + jnp.where(idx == 1, i2.astype(F32), 0.0)
                  + jnp.where(idx == 2, g1, 0.0) + jnp.where(idx == 3, g2, 0.0))


def _router(h, router_t):
    tm = _RT_TM
    return pl.pallas_call(
        _router_kernel,
        grid=(NT // tm,),
        in_specs=[pl.BlockSpec((tm, D_MODEL), lambda i: (i, 0)),
                  pl.BlockSpec((N_EXPERTS, D_MODEL), lambda i: (0, 0))],
        out_specs=pl.BlockSpec((N_EXPERTS, tm), lambda i: (0, i)),
        out_shape=jax.ShapeDtypeStruct((N_EXPERTS, NT), F32),
        compiler_params=_params(1),
        name="moe_router",
    )(h, router_t)


_MOE_TM = 1152
_MOE_CHUNK = 384
_MOE_TF = 256
_MOE_TILES = (TOP_K * NT + N_EXPERTS * (_MOE_TM - 1)) // _MOE_TM
_MOE_ROWS = _MOE_TILES * _MOE_TM


def _moe_routed_kernel(te_ref, nv_ref, src_ref, dst_ref, h_hbm, gate_ref, wg_ref, wu_ref, wd_ref,
                       y_hbm, xs, xb, acc, gsem, ssem):
    del te_ref
    i, f = pl.program_id(0), pl.program_id(1)
    tm = _MOE_TM
    nv = nv_ref[i]

    def gather_copy(tile, slot, r):
        return pltpu.make_async_copy(h_hbm.at[pl.ds(src_ref[tile * tm + r], 1), :],
                                     xs.at[slot, pl.ds(r, 1), :], gsem.at[slot])

    def scatter_copy(tile, r):
        return pltpu.make_async_copy(acc.at[pl.ds(r, 1), :],
                                     y_hbm.at[pl.ds(dst_ref[tile * tm + r], 1), :], ssem.at[0])

    def for_rows(tile, fn):
        def body(r, carry):
            fn(r)
            return carry
        lax.fori_loop(0, nv_ref[tile], body, 0)

    @pl.when(f == 0)
    def _():
        @pl.when(i == 0)
        def _():
            xs[...] = jnp.zeros_like(xs)
            for_rows(0, lambda r: gather_copy(0, 0, r).start())

        @pl.when(i > 0)
        def _():
            for_rows(i - 1, lambda r: scatter_copy(i - 1, r).wait())

        slot = i % 2
        for_rows(i, lambda r: gather_copy(i, slot, r).wait())

        @pl.when(i + 1 < _MOE_TILES)
        def _():
            for_rows(i + 1, lambda r: gather_copy(i + 1, 1 - slot, r).start())

        xb[...] = xs[slot].astype(BF16)

    wg = wg_ref[...].astype(BF16)
    wu = wu_ref[...].astype(BF16)
    wd = wd_ref[...].astype(BF16)
    for c in range(tm // _MOE_CHUNK):
        rows = slice(c * _MOE_CHUNK, (c + 1) * _MOE_CHUNK)

        @pl.when(c * _MOE_CHUNK < nv)
        def _():
            h = xb[rows, :]
            g = jnp.dot(h, wg, preferred_element_type=F32)
            u = jnp.dot(h, wu, preferred_element_type=F32)
            a = (g * jax.nn.sigmoid(g) * u).astype(BF16)
            part = jnp.dot(a, wd, preferred_element_type=F32)

            @pl.when(f == 0)
            def _():
                acc[rows, :] = part

            @pl.when(f > 0)
            def _():
                acc[rows, :] += part

    @pl.when(f == pl.num_programs(1) - 1)
    def _():
        acc[...] = acc[...] * gate_ref[...]
        for_rows(i, lambda r: scatter_copy(i, r).start())

        @pl.when(i == _MOE_TILES - 1)
        def _():
            for_rows(i, lambda r: scatter_copy(i, r).wait())


def _moe(h, router, wg, wu, wd, idx):
    tm, tf = _MOE_TM, _MOE_TF
    nf = D_FF_EXPERT // tf
    r = _router(h, router[idx].T)
    pairs = TOP_K * NT
    e_flat = r[0:TOP_K].astype(jnp.int32).reshape(pairs)
    gate_flat = r[TOP_K:2 * TOP_K].reshape(pairs)
    onehot = (e_flat[:, None] == jnp.arange(N_EXPERTS, dtype=jnp.int32)[None, :]).astype(jnp.int32)
    csum = jnp.cumsum(onehot, axis=0)
    rank = jnp.take_along_axis(csum, e_flat[:, None], axis=1)[:, 0] - 1
    counts = csum[-1]
    tiles_e = (counts + tm - 1) // tm
    tile_end = jnp.cumsum(tiles_e)
    tile_start = tile_end - tiles_e
    pos = tile_start[e_flat] * tm + rank
    pair_id = jnp.arange(pairs, dtype=jnp.int32)
    src = jnp.zeros((_MOE_ROWS,), jnp.int32).at[pos].set(pair_id % NT)
    dst = jnp.zeros((_MOE_ROWS,), jnp.int32).at[pos].set(pair_id)
    gates = jnp.zeros((_MOE_ROWS,), F32).at[pos].set(gate_flat).reshape(_MOE_ROWS, 1)
    tile = jnp.arange(_MOE_TILES, dtype=jnp.int32)
    te = jnp.minimum(jnp.searchsorted(tile_end, tile, side='right').astype(jnp.int32), N_EXPERTS - 1)
    nv = jnp.clip(counts[te] - (tile - tile_start[te]) * tm, 0, tm).astype(jnp.int32)

    def wmap(i, f, te_ref, nv_ref, *_):
        return (idx, te_ref[i], 0, jnp.where(nv_ref[i] > 0, f, nf - 1))

    def wdmap(i, f, te_ref, nv_ref, *_):
        return (idx, te_ref[i], jnp.where(nv_ref[i] > 0, f, nf - 1), 0)

    grid_spec = pltpu.PrefetchScalarGridSpec(
        num_scalar_prefetch=4,
        grid=(_MOE_TILES, nf),
        in_specs=[
            pl.BlockSpec(memory_space=pl.ANY),
            pl.BlockSpec((tm, 1), lambda i, f, *_: (i, 0)),
            pl.BlockSpec((None, None, D_MODEL, tf), wmap),
            pl.BlockSpec((None, None, D_MODEL, tf), wmap),
            pl.BlockSpec((None, None, tf, D_MODEL), wdmap),
        ],
        out_specs=pl.BlockSpec(memory_space=pl.ANY),
        scratch_shapes=[
            pltpu.VMEM((2, tm, D_MODEL), F32),
            pltpu.VMEM((tm, D_MODEL), BF16),
            pltpu.VMEM((tm, D_MODEL), F32),
            pltpu.SemaphoreType.DMA((2,)),
            pltpu.SemaphoreType.DMA((1,)),
        ],
    )
    return pl.pallas_call(
        _moe_routed_kernel,
        grid_spec=grid_spec,
        out_shape=jax.ShapeDtypeStruct((pairs, D_MODEL), F32),
        compiler_params=_params(2),
        name="moe_routed",
    )(te, nv, src, dst, h, gates, wg, wu, wd)


_EPI_TM = 512


def _epilogue_kernel(n_parts, x_ref, *refs):
    y_refs = refs[:n_parts]
    g2_ref, lg_ref, lb_ref, sc_ref, sh_ref, x2_ref, xm_ref = refs[n_parts:]
    y = y_refs[0][...]
    for y_ref in y_refs[1:]:
        y = y + y_ref[...]
    x2 = _layer_norm(DEEPNORM_ALPHA * x_ref[...] + g2_ref[...] * y, lg_ref[...], lb_ref[...])
    x2_ref[...] = x2
    xm_ref[...] = (x2 * (1.0 + sc_ref[...]) + sh_ref[...]).astype(BF16)


def _epilogue(x1, y, mod, ln_g, ln_b, layer):
    tm = _EPI_TM
    n_parts = y.shape[0] // NT
    nxt = min(layer + 1, DEPTH - 1)
    vec = pl.BlockSpec((None, 1, D_MODEL), lambda i: (layer, 0, 0))
    row = pl.BlockSpec((tm, D_MODEL), lambda i: (i, 0))
    parts = [pl.BlockSpec((tm, D_MODEL), functools.partial(lambda p, i: (p * (NT // tm) + i, 0), p))
             for p in range(n_parts)]
    return pl.pallas_call(
        functools.partial(_epilogue_kernel, n_parts),
        grid=(NT // tm,),
        in_specs=[row] + parts + [_mod_spec(layer, 5, tm), vec, vec, _mod_spec(nxt, 1, tm),
                                  _mod_spec(nxt, 0, tm)],
        out_specs=[row, row],
        out_shape=[jax.ShapeDtypeStruct((NT, D_MODEL), F32), jax.ShapeDtypeStruct((NT, D_MODEL), BF16)],
        compiler_params=_params(1),
        name="ffn_epilogue",
    )(x1, *([y] * n_parts), mod, ln_g.reshape(DEPTH, 1, D_MODEL), ln_b.reshape(DEPTH, 1, D_MODEL), mod, mod)


def kernel(x_prompt, x_sample, cache_na_k, cache_na_v, cache_swa_k, cache_swa_v, c, c_ctx, w_mod, b_mod, w_in, w_out, na_rpb, swa_sink, hy_short_w, hy_short_b, hy_w1, hy_b1, hy_freq1, hy_w2, hy_b2, hy_freq2, hy_w3, hy_skip, ln1_g, ln1_b, ln2_g, ln2_b, ffn_w_gate, ffn_w_up, ffn_w_down, moe_router, moe_w_gate, moe_w_up, moe_w_down):
    cond = jnp.concatenate([c_ctx[None, :], c], axis=0)
    mod = _modulation(cond, w_mod, b_mod)
    x, xm = _prologue(x_prompt.reshape(NP, D_MODEL), x_sample.reshape(NS, D_MODEL), mod)

    f3_np, i3_np = _dft_mats()
    f3, i3 = jnp.asarray(f3_np), jnp.asarray(i3_np)
    hy_w1p = jnp.pad(hy_w1, ((0, 0), (0, HY_FFN - HY_POS_DIM), (0, 0)))

    kv_out = []
    for l in range(DEPTH):
        proj = _inproj(xm, w_in, l)
        kv_out.append(proj[:NP])

        oa_c, oc_c = _ctx_attention(proj, swa_sink, l)
        oa_l = _na_latent(proj, cache_na_k, cache_na_v, na_rpb[l], l)
        oc_l = _swa_latent(proj, cache_swa_k, cache_swa_v, swa_sink, l)

        filt = (l, hy_w1p, hy_b1, hy_freq1, hy_w2, hy_b2, hy_freq2, hy_w3)
        g_ctx = _hy_spectrum(*_hy_filter(SEQ, *filt), f3)
        g_lat = _hy_spectrum(*_hy_filter(DEC_SEQ, *filt), f3)
        ob_c = _hy_conv(proj, g_ctx, f3, i3, hy_short_w, hy_short_b, hy_skip, l, SEQ, 0, BATCH, 512)
        ob_l = _hy_conv(proj, g_lat, f3, i3, hy_short_w, hy_short_b, hy_skip, l, DEC_SEQ, NP, DEC_BATCH, 128)

        x1, h = _outproj((oa_c, ob_c, oc_c), (oa_l, ob_l, oc_l), w_out, x, mod, ln1_g, ln1_b, l,
                         BF16 if l % 2 == 0 else F32)
        if l % 2 == 0:
            y = _ffn_dense(h, ffn_w_gate, ffn_w_up, ffn_w_down, l // 2)
        else:
            y = _moe(h, moe_router, moe_w_gate, moe_w_up, moe_w_down, l // 2)
        x, xm = _epilogue(x1, y, mod, ln2_g, ln2_b, l)

    def cache(col, heads):
        parts = [p[:, col:col + heads * HEAD_DIM].reshape(BATCH, SEQ, heads, HEAD_DIM) for p in kv_out]
        return jnp.stack(parts, axis=1)

    return (x[:NP].reshape(BATCH, SEQ, D_MODEL), x[NP:].reshape(DEC_BATCH, DEC_SEQ, D_MODEL),
            cache(COL_KA, NA_HEADS), cache(COL_VA, NA_HEADS),
            cache(COL_KC, SWA_KV_HEADS), cache(COL_VC, SWA_KV_HEADS))
```

```python
import functools
import math

import numpy as np
import jax
import jax.numpy as jnp
from jax import lax
from jax.experimental import pallas as pl
from jax.experimental.pallas import tpu as pltpu

F32 = jnp.float32
BF16 = jnp.bfloat16

D_MODEL = 2048
BATCH = 32
SEQ = 256
DEPTH = 2
DEC_BATCH = 2
DEC_SEQ = 2048
PAST_LEN = 256
GRID_W = 64
GRID_ROWS = DEC_SEQ // GRID_W
HEAD_DIM = 64
ATTN_SCALE = HEAD_DIM ** -0.5
NA_HEADS = 8
NA_WIN_ROWS = 8
NA_WIN_COLS = 16
HY_WIDTH = 1024
HY_ORDER = 2
HY_SHORT = 3
HY_POS_BANDS = 16
HY_POS_DIM = 1 + 2 * HY_POS_BANDS
HY_FFN = 64
HY_FAST_DECAY = 0.3
HY_SLOW_DECAY = 1.5
HY_DECAY_TARGET = 1e-2
HY_MOD_SHIFT = 0.05
SWA_HEADS = 8
SWA_KV_HEADS = 2
SWA_GROUP = SWA_HEADS // SWA_KV_HEADS
SWA_WINDOW = 128
SWA_BLOCK = 128
NA_W = NA_HEADS * HEAD_DIM
SWA_W = SWA_HEADS * HEAD_DIM
SWA_KV_W = SWA_KV_HEADS * HEAD_DIM
MIX_W = NA_W + HY_WIDTH + SWA_W
IN_W = 3 * NA_W + 3 * HY_WIDTH + SWA_W + 2 * SWA_KV_W
D_FF = 7168
N_EXPERTS = 8
TOP_K = 2
D_FF_EXPERT = 7168
ROPE_THETA = 10000.0
LN_EPS = 1e-5
NEG_INF = -1e30
DEEPNORM_ALPHA = (2 * DEPTH) ** 0.25

NP = BATCH * SEQ
NS = DEC_BATCH * DEC_SEQ
NT = NP + NS
N_COND = 1 + DEC_BATCH
COND_PAD = 8

COL_QA, COL_KA, COL_VA = 0, NA_W, 2 * NA_W
COL_HY = 3 * NA_W
COL_QC = COL_HY + 3 * HY_WIDTH
COL_KC = COL_QC + SWA_W
COL_VC = COL_KC + SWA_KV_W

HY_BLK = 256
HY_NFFT = 2 * HY_BLK

LANES = 128
V7X_VMEM_BYTES = 64 * 1024 * 1024
VMEM_LIMIT_BYTES = V7X_VMEM_BYTES * 7 // 8


def _params(grid_rank):
    return pltpu.CompilerParams(dimension_semantics=("arbitrary",) * grid_rank,
                                vmem_limit_bytes=VMEM_LIMIT_BYTES)


def _cond_of_row(row):
    return jnp.maximum((row - NP) // DEC_SEQ + 1, 0)


def _bf16_split_np(a):
    a = np.asarray(a, np.float32)
    hi = a.astype(BF16)
    lo = (a - hi.astype(np.float32)).astype(BF16)
    return hi, lo


_MOD_TN = 512


def _mod_kernel(cb_ref, w_ref, b_ref, o_ref, s_scr):
    @pl.when((pl.program_id(0) == 0) & (pl.program_id(1) == 0))
    def _():
        c = cb_ref[...]
        s_scr[...] = c * jax.nn.sigmoid(c)

    o_ref[...] = jnp.zeros_like(o_ref)
    for n in range(_MOD_TN // LANES):
        sl = slice(n * LANES, (n + 1) * LANES)
        w = w_ref[:, sl]
        for j in range(N_COND):
            acc = jnp.sum(w * s_scr[j], axis=0, keepdims=True)
            o_ref[j:j + 1, sl] = acc + b_ref[:, sl]


def _modulation(cond, w_mod, b_mod):
    cb = jnp.broadcast_to(cond[:, :, None], (N_COND, D_MODEL, LANES))
    n6 = 6 * D_MODEL
    out = pl.pallas_call(
        _mod_kernel,
        grid=(DEPTH, n6 // _MOD_TN),
        in_specs=[
            pl.BlockSpec((N_COND, D_MODEL, LANES), lambda l, j: (0, 0, 0)),
            pl.BlockSpec((None, D_MODEL, _MOD_TN), lambda l, j: (l, 0, j)),
            pl.BlockSpec((None, 1, _MOD_TN), lambda l, j: (l, 0, j)),
        ],
        out_specs=pl.BlockSpec((None, COND_PAD, _MOD_TN), lambda l, j: (l, 0, j)),
        out_shape=jax.ShapeDtypeStruct((DEPTH, COND_PAD, n6), F32),
        scratch_shapes=[pltpu.VMEM((N_COND, D_MODEL, LANES), F32)],
        compiler_params=_params(2),
        name="adaln_mod",
    )(cb, w_mod, b_mod.reshape(DEPTH, 1, n6))
    out = out.reshape(DEPTH, COND_PAD, 6, D_MODEL).transpose(0, 2, 1, 3)
    return out.reshape(DEPTH, 6, COND_PAD, 1, D_MODEL)


def _mod_spec(layer, which, tm):
    return pl.BlockSpec((None, None, None, 1, D_MODEL),
                        lambda i, *_: (layer, which, _cond_of_row(i * tm), 0, 0))


_PRO_TM = 512


def _prologue_kernel(xp_ref, xs_ref, sc_ref, sh_ref, x_ref, xm_ref):
    i = pl.program_id(0)

    def emit(x):
        x_ref[...] = x
        xm_ref[...] = (x * (1.0 + sc_ref[...]) + sh_ref[...]).astype(BF16)

    @pl.when(i < NP // _PRO_TM)
    def _():
        emit(xp_ref[...])

    @pl.when(i >= NP // _PRO_TM)
    def _():
        emit(xs_ref[...])


def _prologue(xp, xs, mod):
    tm = _PRO_TM
    npt = NP // tm
    return pl.pallas_call(
        _prologue_kernel,
        grid=(NT // tm,),
        in_specs=[
            pl.BlockSpec((tm, D_MODEL), lambda i: (jnp.minimum(i, npt - 1), 0)),
            pl.BlockSpec((tm, D_MODEL), lambda i: (jnp.maximum(i - npt, 0), 0)),
            _mod_spec(0, 1, tm),
            _mod_spec(0, 0, tm),
        ],
        out_specs=[pl.BlockSpec((tm, D_MODEL), lambda i: (i, 0)),
                   pl.BlockSpec((tm, D_MODEL), lambda i: (i, 0))],
        out_shape=[jax.ShapeDtypeStruct((NT, D_MODEL), F32),
                   jax.ShapeDtypeStruct((NT, D_MODEL), BF16)],
        compiler_params=_params(1),
        name="prologue",
    )(xp, xs, mod, mod)


_INP_TM = 1024
_INP_TN = 768


def _inproj_kernel(x_ref, w_ref, o_ref):
    o_ref[...] = jnp.dot(x_ref[...], w_ref[...].astype(BF16), preferred_element_type=F32)


def _inproj(xm, w_in, layer):
    tm, tn = _INP_TM, _INP_TN
    return pl.pallas_call(
        _inproj_kernel,
        grid=(NT // tm, IN_W // tn),
        in_specs=[
            pl.BlockSpec((tm, D_MODEL), lambda i, j: (i, 0)),
            pl.BlockSpec((None, D_MODEL, tn), lambda i, j: (layer, 0, j)),
        ],
        out_specs=pl.BlockSpec((tm, tn), lambda i, j: (i, j)),
        out_shape=jax.ShapeDtypeStruct((NT, IN_W), F32),
        compiler_params=_params(2),
        name="inproj",
    )(xm, w_in)


def _attend(q, k, v, bias=None, valid=None, sink=None):
    s = lax.dot_general(q, k, (((1,), (1,)), ((), ())), preferred_element_type=F32)
    if bias is not None:
        s = s + bias
    if valid is not None:
        s = jnp.where(valid, s, NEG_INF)
    m = jnp.max(s, axis=-1, keepdims=True)
    if sink is not None:
        m = jnp.maximum(m, sink)
    p = jnp.exp(s - m)
    denom = jnp.sum(p, axis=-1, keepdims=True)
    if sink is not None:
        denom = denom + jnp.exp(sink - m)
    o = jnp.dot(p.astype(BF16), v, preferred_element_type=F32)
    return o / denom


def _head(x, h):
    return x[:, h * HEAD_DIM:(h + 1) * HEAD_DIM]


def _rope(x, cos, sin_signed):
    width = x.shape[1]
    lane = lax.broadcasted_iota(jnp.int32, x.shape, 1)
    first = (lane % 32) < 16
    partner = jnp.where(first, pltpu.roll(x, width - 16, 1), pltpu.roll(x, 16, 1))
    return x * cos + partner * sin_signed


def _rope_tables(n_heads):
    half = HEAD_DIM // 2
    inv = ROPE_THETA ** (-np.arange(0, half, 2, dtype=np.float64) / half)
    t = np.arange(DEC_SEQ)
    inv = inv.astype(np.float32).astype(np.float64)
    ang_r = (t // GRID_W)[:, None].astype(np.float64) * inv[None, :]
    ang_c = (t % GRID_W)[:, None].astype(np.float64) * inv[None, :]
    cos = np.concatenate([np.cos(ang_r), np.cos(ang_r), np.cos(ang_c), np.cos(ang_c)], -1)
    sin = np.concatenate([-np.sin(ang_r), np.sin(ang_r), -np.sin(ang_c), np.sin(ang_c)], -1)
    cos = np.tile(cos, (1, n_heads)).astype(np.float32)
    sin = np.tile(sin, (1, n_heads)).astype(np.float32)
    return cos, sin


def _ctx_attn_kernel(layer, sink_ref, qa_ref, ka_ref, va_ref, qc_ref, kvc_ref, oa_ref, oc_ref):
    qa = (qa_ref[...] * ATTN_SCALE).astype(BF16)
    ka = ka_ref[...].astype(BF16)
    va = va_ref[...].astype(BF16)
    outs = [_attend(_head(qa, h), _head(ka, h), _head(va, h)) for h in range(NA_HEADS)]
    oa_ref[...] = jnp.concatenate(outs, axis=-1).astype(BF16)

    qc = (qc_ref[...] * ATTN_SCALE).astype(BF16)
    kvc = kvc_ref[...].astype(BF16)
    kc, vc = kvc[:, :SWA_KV_W], kvc[:, SWA_KV_W:]
    outs = []
    for h in range(SWA_HEADS):
        hk = h // SWA_GROUP
        outs.append(_attend(_head(qc, h), _head(kc, hk), _head(vc, hk), sink=sink_ref[layer, h]))
    oc_ref[...] = jnp.concatenate(outs, axis=-1).astype(BF16)


def _ctx_attention(proj, swa_sink, layer):
    blk = lambda w, c: pl.BlockSpec((SEQ, w), lambda b: (b, c // w))
    return pl.pallas_call(
        functools.partial(_ctx_attn_kernel, layer),
        grid=(BATCH,),
        in_specs=[
            pl.BlockSpec(memory_space=pltpu.SMEM),
            blk(NA_W, COL_QA), blk(NA_W, COL_KA), blk(NA_W, COL_VA),
            blk(SWA_W, COL_QC), blk(2 * SWA_KV_W, COL_KC),
        ],
        out_specs=[pl.BlockSpec((SEQ, NA_W), lambda b: (b, 0)),
                   pl.BlockSpec((SEQ, SWA_W), lambda b: (b, 0))],
        out_shape=[jax.ShapeDtypeStruct((NP, NA_W), BF16),
                   jax.ShapeDtypeStruct((NP, SWA_W), BF16)],
        compiler_params=_params(1),
        name="ctx_attn",
    )(swa_sink, proj, proj, proj, proj, proj)


_NA_BAND = NA_WIN_ROWS * GRID_W


def _na_band_start(r):
    return jnp.clip(r - NA_WIN_ROWS // 2, 0, GRID_ROWS - NA_WIN_ROWS)


def _na_kernel(q_ref, k_ref, v_ref, ck_ref, cv_ref, bias_ref, o_ref):
    r = pl.program_id(1)
    start = pl.multiple_of(_na_band_start(r) * GRID_W, GRID_W)
    q = (q_ref[...] * ATTN_SCALE).astype(BF16)
    k = jnp.concatenate([k_ref[pl.ds(start, _NA_BAND), :].astype(BF16), ck_ref[...].astype(BF16)], axis=0)
    v = jnp.concatenate([v_ref[pl.ds(start, _NA_BAND), :].astype(BF16), cv_ref[...].astype(BF16)], axis=0)
    outs = [_attend(_head(q, h), _head(k, h), _head(v, h), bias=bias_ref[h]) for h in range(NA_HEADS)]
    o_ref[...] = jnp.concatenate(outs, axis=-1).astype(BF16)


def _na_bias_tables(rpb_l):
    nc = 2 * NA_WIN_COLS - 1
    col = np.arange(GRID_W)
    dc = np.clip(col[None, :] - col[:, None], -(NA_WIN_COLS - 1), NA_WIN_COLS - 1) + NA_WIN_COLS - 1
    onehot = (dc.reshape(-1)[None, :] == np.arange(nc)[:, None]).astype(np.float32)
    cs = np.clip(col - NA_WIN_COLS // 2, 0, GRID_W - NA_WIN_COLS)
    in_win = (col[None, :] >= cs[:, None]) & (col[None, :] < cs[:, None] + NA_WIN_COLS)
    toe = jnp.einsum('hrd,dn->hrn', rpb_l.astype(F32), jnp.asarray(onehot),
                     precision=lax.Precision.HIGHEST)
    toe = toe.reshape(NA_HEADS, 2 * NA_WIN_ROWS - 1, GRID_W, GRID_W)
    toe = jnp.where(jnp.asarray(in_win)[None, None], toe, NEG_INF)
    strips = jnp.stack([toe[:, o:o + NA_WIN_ROWS] for o in range(NA_WIN_ROWS)], 0)
    strips = strips.transpose(0, 1, 3, 2, 4).reshape(NA_WIN_ROWS, NA_HEADS, GRID_W, _NA_BAND)
    return jnp.pad(strips, ((0, 0), (0, 0), (0, 0), (0, PAST_LEN)))


def _na_latent(proj, cache_k, cache_v, rpb_l, layer):
    bias = _na_bias_tables(rpb_l)
    row0 = NP // GRID_W
    qmap = lambda b, r: (row0 + b * GRID_ROWS + r, 0)
    kvspec = lambda c: pl.BlockSpec((DEC_SEQ, NA_W), lambda b, r: (NP // DEC_SEQ + b, c // NA_W))
    cspec = pl.BlockSpec((None, None, PAST_LEN, NA_W), lambda b, r: (b, layer, 0, 0))

    def bias_map(b, r):
        return (_na_band_start(r) - r + NA_WIN_ROWS - 1, 0, 0, 0)

    return pl.pallas_call(
        _na_kernel,
        grid=(DEC_BATCH, GRID_ROWS),
        in_specs=[
            pl.BlockSpec((GRID_W, NA_W), qmap),
            kvspec(COL_KA), kvspec(COL_VA), cspec, cspec,
            pl.BlockSpec((None, NA_HEADS, GRID_W, _NA_BAND + PAST_LEN), bias_map),
        ],
        out_specs=pl.BlockSpec((GRID_W, NA_W), lambda b, r: (b * GRID_ROWS + r, 0)),
        out_shape=jax.ShapeDtypeStruct((NS, NA_W), BF16),
        compiler_params=_params(2),
        name="na_latent",
    )(proj, proj, proj, cache_k.reshape(DEC_BATCH, DEPTH, PAST_LEN, NA_W),
      cache_v.reshape(DEC_BATCH, DEPTH, PAST_LEN, NA_W), bias)


_SWA_KEYS = 3 * SWA_BLOCK


def _swa_kernel(layer, sink_ref, q_ref, kv_ref, ck_ref, cv_ref, cq_ref, sq_ref, ckt_ref, skt_ref,
                o_ref):
    n = pl.program_id(1)
    start = pl.multiple_of(jnp.clip((n - 1) * SWA_BLOCK, 0, DEC_SEQ - _SWA_KEYS), SWA_BLOCK)
    q = (_rope(q_ref[...], cq_ref[...], sq_ref[...]) * ATTN_SCALE).astype(BF16)
    kv = kv_ref[pl.ds(start, _SWA_KEYS), :]
    kw = _rope(kv[:, :SWA_KV_W], ckt_ref[pl.ds(start, _SWA_KEYS), :], skt_ref[pl.ds(start, _SWA_KEYS), :])
    k = jnp.concatenate([kw.astype(BF16), ck_ref[...].astype(BF16)], axis=0)
    v = jnp.concatenate([kv[:, SWA_KV_W:].astype(BF16), cv_ref[...].astype(BF16)], axis=0)
    nk = _SWA_KEYS + PAST_LEN
    qpos = n * SWA_BLOCK + lax.broadcasted_iota(jnp.int32, (SWA_BLOCK, nk), 0)
    kidx = lax.broadcasted_iota(jnp.int32, (SWA_BLOCK, nk), 1)
    valid = (jnp.abs(qpos - (start + kidx)) <= SWA_WINDOW) | (kidx >= _SWA_KEYS)
    outs = []
    for h in range(SWA_HEADS):
        hk = h // SWA_GROUP
        outs.append(_attend(_head(q, h), _head(k, hk), _head(v, hk), valid=valid, sink=sink_ref[layer, h]))
    o_ref[...] = jnp.concatenate(outs, axis=-1).astype(BF16)


def _swa_latent(proj, cache_k, cache_v, swa_sink, layer):
    cq, sq = _rope_tables(SWA_HEADS)
    ck, sk = _rope_tables(SWA_KV_HEADS)
    row0 = NP // SWA_BLOCK
    nblk = DEC_SEQ // SWA_BLOCK
    qmap = lambda b, n: (row0 + b * nblk + n, COL_QC // SWA_W)
    omap = lambda b, n: (b * nblk + n, 0)
    cspec = pl.BlockSpec((None, None, PAST_LEN, SWA_KV_W), lambda b, n: (b, layer, 0, 0))
    tq = pl.BlockSpec((SWA_BLOCK, SWA_W), lambda b, n: (n, 0))
    tk = pl.BlockSpec((DEC_SEQ, SWA_KV_W), lambda b, n: (0, 0))
    return pl.pallas_call(
        functools.partial(_swa_kernel, layer),
        grid=(DEC_BATCH, nblk),
        in_specs=[
            pl.BlockSpec(memory_space=pltpu.SMEM),
            pl.BlockSpec((SWA_BLOCK, SWA_W), qmap),
            pl.BlockSpec((DEC_SEQ, 2 * SWA_KV_W), lambda b, n: (NP // DEC_SEQ + b, COL_KC // (2 * SWA_KV_W))),
            cspec, cspec, tq, tq, tk, tk,
        ],
        out_specs=pl.BlockSpec((SWA_BLOCK, SWA_W), omap),
        out_shape=jax.ShapeDtypeStruct((NS, SWA_W), BF16),
        compiler_params=_params(2),
        name="swa_latent",
    )(swa_sink, proj, proj, cache_k.reshape(DEC_BATCH, DEPTH, PAST_LEN, SWA_KV_W),
      cache_v.reshape(DEC_BATCH, DEPTH, PAST_LEN, SWA_KV_W),
      jnp.asarray(cq), jnp.asarray(sq), jnp.asarray(ck), jnp.asarray(sk))


def _dft_mats():
    bk, n = HY_BLK, HY_NFFT
    s = np.arange(bk)
    ang = 2.0 * np.pi * ((s[:, None] * s[None, :]) % n) / n
    fwd = np.zeros((n, bk))
    fwd[:bk] = np.cos(ang)
    fwd[bk:] = -np.sin(ang)
    fwd[bk] = (-1.0) ** s
    inv = np.zeros((bk, n))
    inv[:, :bk] = (2.0 / n) * np.cos(ang)
    inv[:, 0] = 1.0 / n
    inv[:, bk:] = -(2.0 / n) * np.sin(ang)
    inv[:, bk] = (1.0 / n) * (-1.0) ** s
    fh, fl = _bf16_split_np(fwd)
    ih, il = _bf16_split_np(inv)
    return np.concatenate([fh, fh, fl], axis=1), np.concatenate([ih, ih, il], axis=1)


def _split3(x):
    hi = x.astype(BF16)
    lo = (x - hi.astype(F32)).astype(BF16)
    return jnp.concatenate([hi, lo, hi], axis=0)


def _hy_features(L):
    pos = np.arange(L, dtype=np.float64)
    t = (pos.astype(np.float32) / np.float32(max(L - 1, 1))).astype(np.float64)
    bands = np.linspace(1e-4, HY_POS_BANDS - 1, HY_POS_BANDS, dtype=np.float32).astype(np.float64)
    ang = np.float64(np.float32(2.0 * math.pi / L)) * pos[:, None] * bands[None, :]
    feat = np.concatenate([t[:, None], np.cos(ang), -np.sin(ang)], -1)
    feat = np.pad(feat, ((0, 0), (0, HY_FFN - HY_POS_DIM)))
    deltas = np.abs(np.linspace(math.log(HY_DECAY_TARGET) / HY_SLOW_DECAY,
                                math.log(HY_DECAY_TARGET) / HY_FAST_DECAY, HY_WIDTH, dtype=np.float32))
    window = np.exp(-t[:, None] * deltas[None, :].astype(np.float64)) + HY_MOD_SHIFT
    return feat.astype(np.float32), window.astype(np.float32)


_HYF_TC = 512


def _hy_filter_kernel(feat_ref, w1_ref, b1_ref, f1_ref, w2_ref, b2_ref, f2_ref, w3f_ref, w3b_ref, win_ref,
                      hf_ref, hb_ref):
    hp = lax.Precision.HIGHEST
    z = jnp.dot(feat_ref[...], w1_ref[...], precision=hp, preferred_element_type=F32) + b1_ref[...]
    h = jnp.sin(f1_ref[...] * z)
    z = jnp.dot(h, w2_ref[...], precision=hp, preferred_element_type=F32) + b2_ref[...]
    h = jnp.sin(f2_ref[...] * z)
    win = win_ref[...]
    af = jnp.dot(h, w3f_ref[...], precision=hp, preferred_element_type=F32) * win
    ab = jnp.dot(h, w3b_ref[...], precision=hp, preferred_element_type=F32) * win
    nrm = jnp.sum(jnp.abs(af), axis=0, keepdims=True) + jnp.sum(jnp.abs(ab), axis=0, keepdims=True)
    hf_ref[...] = af / nrm
    hb_ref[...] = ab / nrm


def _hy_filter(L, layer, hy_w1p, hy_b1, hy_freq1, hy_w2, hy_b2, hy_freq2, hy_w3):
    feat, window = _hy_features(L)
    tc = _HYF_TC
    ncc = HY_WIDTH // tc
    ow = HY_ORDER * HY_WIDTH
    small = lambda shape: pl.BlockSpec((None,) + shape, lambda o, c: (layer,) + (0,) * len(shape))
    w3spec = lambda d: pl.BlockSpec((None, HY_FFN, tc), lambda o, c: (layer, 0, (d * HY_ORDER + o) * ncc + c))
    ospec = pl.BlockSpec((L, tc), lambda o, c: (0, o * ncc + c))
    vec = lambda a: a.reshape(DEPTH, 1, HY_FFN)
    return pl.pallas_call(
        _hy_filter_kernel,
        grid=(HY_ORDER, ncc),
        in_specs=[
            pl.BlockSpec((L, HY_FFN), lambda o, c: (0, 0)),
            small((HY_FFN, HY_FFN)), small((1, HY_FFN)), small((1, HY_FFN)),
            small((HY_FFN, HY_FFN)), small((1, HY_FFN)), small((1, HY_FFN)),
            w3spec(0), w3spec(1),
            pl.BlockSpec((L, tc), lambda o, c: (0, c)),
        ],
        out_specs=[ospec, ospec],
        out_shape=[jax.ShapeDtypeStruct((L, ow), F32), jax.ShapeDtypeStruct((L, ow), F32)],
        compiler_params=_params(2),
        name=f"hy_filter_{L}",
    )(jnp.asarray(feat), hy_w1p, vec(hy_b1), vec(hy_freq1), hy_w2, vec(hy_b2), vec(hy_freq2),
      hy_w3, hy_w3, jnp.asarray(window))


_HYS_TC = 256


def _hy_spectrum_kernel(nb, hf_ref, hb_ref, f3_ref, g_ref):
    n, bk = HY_NFFT, HY_BLK
    row = lax.broadcasted_iota(jnp.int32, (n, 1), 0)
    ones_lo = (row <= bk).astype(F32)
    sgn = jnp.where(row % 2 == 0, 1.0, -1.0).astype(F32)
    conj = jnp.where(row > bk, -1.0, 1.0).astype(F32)
    f3 = f3_ref[...]

    def spectra(h_ref, drop_lag0):
        first, tail = [], []
        for b in range(nb):
            blk = h_ref[b * bk:(b + 1) * bk, :]
            t = jnp.dot(f3, _split3(blk), preferred_element_type=F32)
            head = ones_lo * blk[0:1, :]
            if b == 0 and drop_lag0:
                t = t - head
                tail.append(sgn * t)
            else:
                tail.append(sgn * (t - head))
            first.append(t)
        return [first[d] + (tail[d - 1] if d >= 1 else 0.0) for d in range(nb)]

    gf = spectra(hf_ref, False)
    gb = spectra(hb_ref, True)
    g_ref[nb - 1] = gf[0] + conj * gb[0]
    for d in range(1, nb):
        g_ref[nb - 1 + d] = gf[d]
        g_ref[nb - 1 - d] = conj * gb[d]


def _hy_spectrum(hf, hb, f3):
    L = hf.shape[0]
    nb = L // HY_BLK
    ow = HY_ORDER * HY_WIDTH
    tc = _HYS_TC
    return pl.pallas_call(
        functools.partial(_hy_spectrum_kernel, nb),
        grid=(ow // tc,),
        in_specs=[
            pl.BlockSpec((L, tc), lambda c: (0, c)),
            pl.BlockSpec((L, tc), lambda c: (0, c)),
            pl.BlockSpec((HY_NFFT, 3 * HY_BLK), lambda c: (0, 0)),
        ],
        out_specs=pl.BlockSpec((2 * nb - 1, HY_NFFT, tc), lambda c: (0, 0, c)),
        out_shape=jax.ShapeDtypeStruct((2 * nb - 1, HY_NFFT, ow), F32),
        compiler_params=_params(1),
        name=f"hy_spectrum_{L}",
    )(hf, hb, f3)


def _spec_mul(x, g):
    bk = HY_BLK
    xa, xb = x[:bk], x[bk:]
    ga, gb = g[:bk], g[bk:]
    row0 = lax.broadcasted_iota(jnp.int32, xa.shape, 0) == 0
    bb = xb * gb
    pa = xa * ga - jnp.where(row0, 0.0, bb)
    pb = jnp.where(row0, bb, xa * gb + xb * ga)
    return jnp.concatenate([pa, pb], axis=0)


def _short_conv(u, w, b):
    L = u.shape[0]
    row = lax.broadcasted_iota(jnp.int32, u.shape, 0)
    prev = jnp.where(row == 0, 0.0, pltpu.roll(u, 1, 0))
    nxt = jnp.where(row == L - 1, 0.0, pltpu.roll(u, L - 1, 0))
    return prev * w[0:1] + u * w[1:2] + nxt * w[2:3] + b


def _hy_conv_kernel(nb, uv_ref, u1_ref, u2_ref, wv_ref, w1_ref, w2_ref, bv_ref, b1_ref, b2_ref, skip_ref,
                    g0_ref, g1_ref, f3_ref, i3_ref, o_ref):
    bk = HY_BLK
    v = _short_conv(uv_ref[...], wv_ref[...], bv_ref[...])
    gates = (_short_conv(u1_ref[...], w1_ref[...], b1_ref[...]),
             _short_conv(u2_ref[...], w2_ref[...], b2_ref[...]))
    cc = v.shape[1]
    f3 = f3_ref[...]
    i3 = i3_ref[...]
    z = v
    for o, g_ref in enumerate((g0_ref, g1_ref)):
        zcat = jnp.concatenate([z[j * bk:(j + 1) * bk] for j in range(nb)], axis=1)
        x = jnp.dot(f3, _split3(zcat), preferred_element_type=F32)
        prods = []
        for i in range(nb):
            acc = None
            for j in range(nb):
                t = _spec_mul(x[:, j * cc:(j + 1) * cc], g_ref[i - j + nb - 1])
                acc = t if acc is None else acc + t
            prods.append(acc)
        pcat = jnp.concatenate(prods, axis=1)
        y = jnp.dot(i3, _split3(pcat), preferred_element_type=F32)
        y = jnp.concatenate([y[:, i * cc:(i + 1) * cc] for i in range(nb)], axis=0)
        z = gates[o] * (y + skip_ref[o:o + 1, :] * z)
    o_ref[...] = z.astype(BF16)


def _hy_conv(proj, g, f3, i3, short_w, short_b, skip, layer, L, row0, nbatch, cc):
    nb = L // HY_BLK
    ncc = HY_WIDTH // cc
    rb0 = row0 // L
    ucol = lambda part: pl.BlockSpec((L, cc), lambda c, b: (rb0 + b, (COL_HY + part * HY_WIDTH) // cc + c))
    wcol = lambda part: pl.BlockSpec((None, HY_SHORT, cc), lambda c, b: (layer, 0, part * ncc + c))
    bcol = lambda part: pl.BlockSpec((None, 1, cc), lambda c, b: (layer, 0, part * ncc + c))
    gspec = lambda o: pl.BlockSpec((2 * nb - 1, HY_NFFT, cc), lambda c, b: (0, 0, o * ncc + c))
    in_specs = [
        ucol(0), ucol(1), ucol(2), wcol(0), wcol(1), wcol(2), bcol(0), bcol(1), bcol(2),
        pl.BlockSpec((None, HY_ORDER, cc), lambda c, b: (layer, 0, c)),
        gspec(0), gspec(1),
        pl.BlockSpec((HY_NFFT, 3 * HY_BLK), lambda c, b: (0, 0)),
        pl.BlockSpec((HY_BLK, 3 * HY_NFFT), lambda c, b: (0, 0)),
    ]
    args = [proj, proj, proj, short_w, short_w, short_w] + [short_b.reshape(DEPTH, 1, 3 * HY_WIDTH)] * 3 + [
        skip, g, g, f3, i3]
    return pl.pallas_call(
        functools.partial(_hy_conv_kernel, nb),
        grid=(ncc, nbatch),
        in_specs=in_specs,
        out_specs=pl.BlockSpec((L, cc), lambda c, b: (b, c)),
        out_shape=jax.ShapeDtypeStruct((nbatch * L, HY_WIDTH), BF16),
        compiler_params=_params(2),
        name=f"hy_conv_{L}",
    )(*args)


_OUT_TM = 512
_OUT_TK = 512


def _layer_norm(x, g, b):
    mu = jnp.mean(x, axis=-1, keepdims=True)
    xc = x - mu
    var = jnp.mean(xc * xc, axis=-1, keepdims=True)
    return xc * lax.rsqrt(var + LN_EPS) * g + b


def _outproj_kernel(oa_c, ob_c, oc_c, oa_l, ob_l, oc_l, w_ref, x_ref, g1_ref, lg_ref, lb_ref, sc_ref, sh_ref,
                    x1_ref, h_ref, acc_ref):
    i, k = pl.program_id(0), pl.program_id(1)
    w = w_ref[...].astype(BF16)
    is_ctx = i < NP // _OUT_TM

    def mixer(c_ref, l_ref):
        return jnp.where(is_ctx, c_ref[...], l_ref[...])

    @pl.when(k == 0)
    def _():
        acc_ref[...] = jnp.dot(mixer(oa_c, oa_l), w, preferred_element_type=F32)

    @pl.when((k == 1) | (k == 2))
    def _():
        acc_ref[...] += jnp.dot(mixer(ob_c, ob_l), w, preferred_element_type=F32)

    @pl.when(k == 3)
    def _():
        mix = acc_ref[...] + jnp.dot(mixer(oc_c, oc_l), w, preferred_element_type=F32)
        x1 = _layer_norm(DEEPNORM_ALPHA * x_ref[...] + g1_ref[...] * mix, lg_ref[...], lb_ref[...])
        x1_ref[...] = x1
        h_ref[...] = (x1 * (1.0 + sc_ref[...]) + sh_ref[...]).astype(h_ref.dtype)


def _outproj(ctx_mix, lat_mix, w_out, x, mod, ln_g, ln_b, layer, h_dtype):
    tm, tk = _OUT_TM, _OUT_TK
    npt = NP // tm
    vec = pl.BlockSpec((None, 1, D_MODEL), lambda i, k: (layer, 0, 0))
    crow = lambda i: jnp.minimum(i, npt - 1)
    lrow = lambda i: jnp.maximum(i - npt, 0)
    kb = lambda k: jnp.clip(k - 1, 0, 1)
    return pl.pallas_call(
        _outproj_kernel,
        grid=(NT // tm, MIX_W // tk),
        in_specs=[
            pl.BlockSpec((tm, tk), lambda i, k: (crow(i), 0)),
            pl.BlockSpec((tm, tk), lambda i, k: (crow(i), kb(k))),
            pl.BlockSpec((tm, tk), lambda i, k: (crow(i), 0)),
            pl.BlockSpec((tm, tk), lambda i, k: (lrow(i), 0)),
            pl.BlockSpec((tm, tk), lambda i, k: (lrow(i), kb(k))),
            pl.BlockSpec((tm, tk), lambda i, k: (lrow(i), 0)),
            pl.BlockSpec((None, tk, D_MODEL), lambda i, k: (layer, k, 0)),
            pl.BlockSpec((tm, D_MODEL), lambda i, k: (i, 0)),
            _mod_spec(layer, 2, tm), vec, vec, _mod_spec(layer, 4, tm), _mod_spec(layer, 3, tm),
        ],
        out_specs=[pl.BlockSpec((tm, D_MODEL), lambda i, k: (i, 0)),
                   pl.BlockSpec((tm, D_MODEL), lambda i, k: (i, 0))],
        out_shape=[jax.ShapeDtypeStruct((NT, D_MODEL), F32), jax.ShapeDtypeStruct((NT, D_MODEL), h_dtype)],
        scratch_shapes=[pltpu.VMEM((tm, D_MODEL), F32)],
        compiler_params=_params(2),
        name="outproj_ln",
    )(*ctx_mix, *lat_mix, w_out, x, mod, ln_g.reshape(DEPTH, 1, D_MODEL), ln_b.reshape(DEPTH, 1, D_MODEL),
      mod, mod)


_FFN_TM = 1024
_FFN_TF = 256


def _swiglu_partial(h, wg_ref, wu_ref, wd_ref):
    g = jnp.dot(h, wg_ref[...].astype(BF16), preferred_element_type=F32)
    u = jnp.dot(h, wu_ref[...].astype(BF16), preferred_element_type=F32)
    a = (g * jax.nn.sigmoid(g) * u).astype(BF16)
    return jnp.dot(a, wd_ref[...].astype(BF16), preferred_element_type=F32)


def _ffn_kernel(h_ref, wg_ref, wu_ref, wd_ref, y_ref):
    @pl.when(pl.program_id(1) == 0)
    def _():
        y_ref[...] = jnp.zeros_like(y_ref)

    y_ref[...] += _swiglu_partial(h_ref[...], wg_ref, wu_ref, wd_ref)


def _ffn_dense(h, wg, wu, wd, idx):
    tm, tf = _FFN_TM, _FFN_TF
    return pl.pallas_call(
        _ffn_kernel,
        grid=(NT // tm, D_FF // tf),
        in_specs=[
            pl.BlockSpec((tm, D_MODEL), lambda i, f: (i, 0)),
            pl.BlockSpec((None, D_MODEL, tf), lambda i, f: (idx, 0, f)),
            pl.BlockSpec((None, D_MODEL, tf), lambda i, f: (idx, 0, f)),
            pl.BlockSpec((None, tf, D_MODEL), lambda i, f: (idx, f, 0)),
        ],
        out_specs=pl.BlockSpec((tm, D_MODEL), lambda i, f: (i, 0)),
        out_shape=jax.ShapeDtypeStruct((NT, D_MODEL), F32),
        compiler_params=_params(2),
        name="ffn_dense",
    )(h, wg, wu, wd)


_RT_TM = 1024


def _router_kernel(h_ref, rt_ref, r_ref):
    logits = lax.dot_general(rt_ref[...].astype(BF16), h_ref[...].astype(BF16), (((1,), (1,)), ((), ())),
                             preferred_element_type=F32)
    idx = lax.broadcasted_iota(jnp.int32, logits.shape, 0)
    m1 = jnp.max(logits, axis=0, keepdims=True)
    i1 = jnp.min(jnp.where(logits == m1, idx, N_EXPERTS), axis=0, keepdims=True)
    rest = jnp.where(idx == i1, -jnp.inf, logits)
    m2 = jnp.max(rest, axis=0, keepdims=True)
    i2 = jnp.min(jnp.where(rest == m2, idx, N_EXPERTS), axis=0, keepdims=True)
    e2 = jnp.exp(m2 - m1)
    g1 = 1.0 / (1.0 + e2)
    g2 = e2 / (1.0 + e2)
    r_ref[...] = (jnp.where(idx == 0, i1.astype(F32), 0.0) + jnp.where(idx == 1, i2.astype(F32), 0.0)
                  + jnp.where(idx == 2, g1, 0.0) + jnp.where(idx == 3, g2, 0.0))


def _router(h, router_t):
    tm = _RT_TM
    return pl.pallas_call(
        _router_kernel,
        grid=(NT // tm,),
        in_specs=[pl.BlockSpec((tm, D_MODEL), lambda i: (i, 0)),
                  pl.BlockSpec((N_EXPERTS, D_MODEL), lambda i: (0, 0))],
        out_specs=pl.BlockSpec((N_EXPERTS, tm), lambda i: (0, i)),
        out_shape=jax.ShapeDtypeStruct((N_EXPERTS, NT), F32),
        compiler_params=_params(1),
        name="moe_router",
    )(h, router_t)


_MOE_TM = 1152
_MOE_CHUNK = 384
_MOE_TF = 256
_MOE_TILES = (TOP_K * NT + N_EXPERTS * (_MOE_TM - 1)) // _MOE_TM
_MOE_ROWS = _MOE_TILES * _MOE_TM


def _moe_routed_kernel(te_ref, nv_ref, src_ref, dst_ref, h_hbm, gate_ref, wg_ref, wu_ref, wd_ref,
                       y_hbm, xs, xb, acc, gsem, ssem):
    del te_ref
    i, f = pl.program_id(0), pl.program_id(1)
    tm = _MOE_TM
    nv = nv_ref[i]

    def gather_copy(tile, slot, r):
        return pltpu.make_async_copy(h_hbm.at[pl.ds(src_ref[tile * tm + r], 1), :],
                                     xs.at[slot, pl.ds(r, 1), :], gsem.at[slot])

    def scatter_copy(tile, r):
        return pltpu.make_async_copy(acc.at[pl.ds(r, 1), :],
                                     y_hbm.at[pl.ds(dst_ref[tile * tm + r], 1), :], ssem.at[0])

    def for_rows(tile, fn):
        def body(r, carry):
            fn(r)
            return carry
        lax.fori_loop(0, nv_ref[tile], body, 0)

    @pl.when(f == 0)
    def _():
        @pl.when(i == 0)
        def _():
            xs[...] = jnp.zeros_like(xs)
            for_rows(0, lambda r: gather_copy(0, 0, r).start())

        @pl.when(i > 0)
        def _():
            for_rows(i - 1, lambda r: scatter_copy(i - 1, r).wait())

        slot = i % 2
        for_rows(i, lambda r: gather_copy(i, slot, r).wait())

        @pl.when(i + 1 < _MOE_TILES)
        def _():
            for_rows(i + 1, lambda r: gather_copy(i + 1, 1 - slot, r).start())

        xb[...] = xs[slot].astype(BF16)
        acc[...] = jnp.zeros_like(acc)

    wg = wg_ref[...].astype(BF16)
    wu = wu_ref[...].astype(BF16)
    wd = wd_ref[...].astype(BF16)
    for c in range(tm // _MOE_CHUNK):
        rows = slice(c * _MOE_CHUNK, (c + 1) * _MOE_CHUNK)

        @pl.when(c * _MOE_CHUNK < nv)
        def _():
            h = xb[rows, :]
            g = jnp.dot(h, wg, preferred_element_type=F32)
            u = jnp.dot(h, wu, preferred_element_type=F32)
            a = (g * jax.nn.sigmoid(g) * u).astype(BF16)
            acc[rows, :] += jnp.dot(a, wd, preferred_element_type=F32)

    @pl.when(f == pl.num_programs(1) - 1)
    def _():
        acc[...] = acc[...] * gate_ref[...]
        for_rows(i, lambda r: scatter_copy(i, r).start())

        @pl.when(i == _MOE_TILES - 1)
        def _():
            for_rows(i, lambda r: scatter_copy(i, r).wait())


def _moe(h, router, wg, wu, wd, idx):
    tm, tf = _MOE_TM, _MOE_TF
    nf = D_FF_EXPERT // tf
    r = _router(h, router[idx].T)
    pairs = TOP_K * NT
    e_flat = r[0:TOP_K].astype(jnp.int32).reshape(pairs)
    gate_flat = r[TOP_K:2 * TOP_K].reshape(pairs)
    onehot = (e_flat[:, None] == jnp.arange(N_EXPERTS, dtype=jnp.int32)[None, :]).astype(jnp.int32)
    csum = jnp.cumsum(onehot, axis=0)
    rank = jnp.take_along_axis(csum, e_flat[:, None], axis=1)[:, 0] - 1
    counts = csum[-1]
    tiles_e = (counts + tm - 1) // tm
    tile_end = jnp.cumsum(tiles_e)
    tile_start = tile_end - tiles_e
    pos = tile_start[e_flat] * tm + rank
    pair_id = jnp.arange(pairs, dtype=jnp.int32)
    dst = jnp.zeros((_MOE_ROWS,), jnp.int32).at[pos].set(pair_id, unique_indices=True)
    src = dst % NT
    gates = gate_flat[dst].reshape(_MOE_ROWS, 1)
    tile = jnp.arange(_MOE_TILES, dtype=jnp.int32)
    te = jnp.minimum(jnp.sum((tile[:, None] >= tile_end[None, :]).astype(jnp.int32), axis=1), N_EXPERTS - 1)
    nv = jnp.clip(counts[te] - (tile - tile_start[te]) * tm, 0, tm).astype(jnp.int32)

    def wmap(i, f, te_ref, nv_ref, *_):
        return (idx, te_ref[i], 0, jnp.where(nv_ref[i] > 0, f, nf - 1))

    def wdmap(i, f, te_ref, nv_ref, *_):
        return (idx, te_ref[i], jnp.where(nv_ref[i] > 0, f, nf - 1), 0)

    grid_spec = pltpu.PrefetchScalarGridSpec(
        num_scalar_prefetch=4,
        grid=(_MOE_TILES, nf),
        in_specs=[
            pl.BlockSpec(memory_space=pl.ANY),
            pl.BlockSpec((tm, 1), lambda i, f, *_: (i, 0)),
            pl.BlockSpec((None, None, D_MODEL, tf), wmap),
            pl.BlockSpec((None, None, D_MODEL, tf), wmap),
            pl.BlockSpec((None, None, tf, D_MODEL), wdmap),
        ],
        out_specs=pl.BlockSpec(memory_space=pl.ANY),
        scratch_shapes=[
            pltpu.VMEM((2, tm, D_MODEL), F32),
            pltpu.VMEM((tm, D_MODEL), BF16),
            pltpu.VMEM((tm, D_MODEL), F32),
            pltpu.SemaphoreType.DMA((2,)),
            pltpu.SemaphoreType.DMA((1,)),
        ],
    )
    return pl.pallas_call(
        _moe_routed_kernel,
        grid_spec=grid_spec,
        out_shape=jax.ShapeDtypeStruct((pairs, D_MODEL), F32),
        compiler_params=_params(2),
        name="moe_routed",
    )(te, nv, src, dst, h, gates, wg, wu, wd)


_EPI_TM = 512


def _epilogue_kernel(n_parts, x_ref, *refs):
    y_refs = refs[:n_parts]
    g2_ref, lg_ref, lb_ref, sc_ref, sh_ref, x2_ref, xm_ref = refs[n_parts:]
    y = y_refs[0][...]
    for y_ref in y_refs[1:]:
        y = y + y_ref[...]
    x2 = _layer_norm(DEEPNORM_ALPHA * x_ref[...] + g2_ref[...] * y, lg_ref[...], lb_ref[...])
    x2_ref[...] = x2
    xm_ref[...] = (x2 * (1.0 + sc_ref[...]) + sh_ref[...]).astype(BF16)


def _epilogue(x1, y, mod, ln_g, ln_b, layer):
    tm = _EPI_TM
    n_parts = y.shape[0] // NT
    nxt = min(layer + 1, DEPTH - 1)
    vec = pl.BlockSpec((None, 1, D_MODEL), lambda i: (layer, 0, 0))
    row = pl.BlockSpec((tm, D_MODEL), lambda i: (i, 0))
    parts = [pl.BlockSpec((tm, D_MODEL), functools.partial(lambda p, i: (p * (NT // tm) + i, 0), p))
             for p in range(n_parts)]
    return pl.pallas_call(
        functools.partial(_epilogue_kernel, n_parts),
        grid=(NT // tm,),
        in_specs=[row] + parts + [_mod_spec(layer, 5, tm), vec, vec, _mod_spec(nxt, 1, tm),
                                  _mod_spec(nxt, 0, tm)],
        out_specs=[row, row],
        out_shape=[jax.ShapeDtypeStruct((NT, D_MODEL), F32), jax.ShapeDtypeStruct((NT, D_MODEL), BF16)],
        compiler_params=_params(1),
        name="ffn_epilogue",
    )(x1, *([y] * n_parts), mod, ln_g.reshape(DEPTH, 1, D_MODEL), ln_b.reshape(DEPTH, 1, D_MODEL), mod, mod)


def kernel(x_prompt, x_sample, cache_na_k, cache_na_v, cache_swa_k, cache_swa_v, c, c_ctx, w_mod, b_mod, w_in, w_out, na_rpb, swa_sink, hy_short_w, hy_short_b, hy_w1, hy_b1, hy_freq1, hy_w2, hy_b2, hy_freq2, hy_w3, hy_skip, ln1_g, ln1_b, ln2_g, ln2_b, ffn_w_gate, ffn_w_up, ffn_w_down, moe_router, moe_w_gate, moe_w_up, moe_w_down):
    cond = jnp.concatenate([c_ctx[None, :], c], axis=0)
    mod = _modulation(cond, w_mod, b_mod)
    x, xm = _prologue(x_prompt.reshape(NP, D_MODEL), x_sample.reshape(NS, D_MODEL), mod)

    f3_np, i3_np = _dft_mats()
    f3, i3 = jnp.asarray(f3_np), jnp.asarray(i3_np)
    hy_w1p = jnp.pad(hy_w1, ((0, 0), (0, HY_FFN - HY_POS_DIM), (0, 0)))

    kv_out = []
    for l in range(DEPTH):
        proj = _inproj(xm, w_in, l)
        kv_out.append(proj[:NP])

        oa_c, oc_c = _ctx_attention(proj, swa_sink, l)
        oa_l = _na_latent(proj, cache_na_k, cache_na_v, na_rpb[l], l)
        oc_l = _swa_latent(proj, cache_swa_k, cache_swa_v, swa_sink, l)

        filt = (l, hy_w1p, hy_b1, hy_freq1, hy_w2, hy_b2, hy_freq2, hy_w3)
        g_ctx = _hy_spectrum(*_hy_filter(SEQ, *filt), f3)
        g_lat = _hy_spectrum(*_hy_filter(DEC_SEQ, *filt), f3)
        ob_c = _hy_conv(proj, g_ctx, f3, i3, hy_short_w, hy_short_b, hy_skip, l, SEQ, 0, BATCH, 512)
        ob_l = _hy_conv(proj, g_lat, f3, i3, hy_short_w, hy_short_b, hy_skip, l, DEC_SEQ, NP, DEC_BATCH, 128)

        x1, h = _outproj((oa_c, ob_c, oc_c), (oa_l, ob_l, oc_l), w_out, x, mod, ln1_g, ln1_b, l,
                         BF16 if l % 2 == 0 else F32)
        if l % 2 == 0:
            y = _ffn_dense(h, ffn_w_gate, ffn_w_up, ffn_w_down, l // 2)
        else:
            y = _moe(h, moe_router, moe_w_gate, moe_w_up, moe_w_down, l // 2)
        x, xm = _epilogue(x1, y, mod, ln2_g, ln2_b, l)

    def cache(col, heads):
        parts = [p[:, col:col + heads * HEAD_DIM].reshape(BATCH, SEQ, heads, HEAD_DIM) for p in kv_out]
        return jnp.stack(parts, axis=1)

    return (x[:NP].reshape(BATCH, SEQ, D_MODEL), x[NP:].reshape(DEC_BATCH, DEC_SEQ, D_MODEL),
            cache(COL_KA, NA_HEADS), cache(COL_VA, NA_HEADS),
            cache(COL_KC, SWA_KV_HEADS), cache(COL_VC, SWA_KV_HEADS))
```

```python
import functools
import math

import numpy as np
import jax
import jax.numpy as jnp
from jax import lax
from jax.experimental import pallas as pl
from jax.experimental.pallas import tpu as pltpu

F32 = jnp.float32
BF16 = jnp.bfloat16

D_MODEL = 2048
BATCH = 32
SEQ = 256
DEPTH = 2
DEC_BATCH = 2
DEC_SEQ = 2048
PAST_LEN = 256
GRID_W = 64
GRID_ROWS = DEC_SEQ // GRID_W
HEAD_DIM = 64
ATTN_SCALE = HEAD_DIM ** -0.5
NA_HEADS = 8
NA_WIN_ROWS = 8
NA_WIN_COLS = 16
HY_WIDTH = 1024
HY_ORDER = 2
HY_SHORT = 3
HY_POS_BANDS = 16
HY_POS_DIM = 1 + 2 * HY_POS_BANDS
HY_FFN = 64
HY_FAST_DECAY = 0.3
HY_SLOW_DECAY = 1.5
HY_DECAY_TARGET = 1e-2
HY_MOD_SHIFT = 0.05
SWA_HEADS = 8
SWA_KV_HEADS = 2
SWA_GROUP = SWA_HEADS // SWA_KV_HEADS
SWA_WINDOW = 128
SWA_BLOCK = 128
NA_W = NA_HEADS * HEAD_DIM
SWA_W = SWA_HEADS * HEAD_DIM
SWA_KV_W = SWA_KV_HEADS * HEAD_DIM
MIX_W = NA_W + HY_WIDTH + SWA_W
IN_W = 3 * NA_W + 3 * HY_WIDTH + SWA_W + 2 * SWA_KV_W
D_FF = 7168
N_EXPERTS = 8
TOP_K = 2
D_FF_EXPERT = 7168
ROPE_THETA = 10000.0
LN_EPS = 1e-5
NEG_INF = -1e30
DEEPNORM_ALPHA = (2 * DEPTH) ** 0.25

NP = BATCH * SEQ
NS = DEC_BATCH * DEC_SEQ
NT = NP + NS
N_COND = 1 + DEC_BATCH
COND_PAD = 8

COL_QA, COL_KA, COL_VA = 0, NA_W, 2 * NA_W
COL_HY = 3 * NA_W
COL_QC = COL_HY + 3 * HY_WIDTH
COL_KC = COL_QC + SWA_W
COL_VC = COL_KC + SWA_KV_W

HY_BLK = 256
HY_NFFT = 2 * HY_BLK

LANES = 128
V7X_VMEM_BYTES = 64 * 1024 * 1024
VMEM_LIMIT_BYTES = V7X_VMEM_BYTES * 7 // 8


def _params(grid_rank, vmem_limit_bytes=VMEM_LIMIT_BYTES):
    return pltpu.CompilerParams(dimension_semantics=("arbitrary",) * grid_rank,
                                vmem_limit_bytes=vmem_limit_bytes)


def _cond_of_row(row):
    return jnp.maximum((row - NP) // DEC_SEQ + 1, 0)


def _bf16_split_np(a):
    a = np.asarray(a, np.float32)
    hi = a.astype(BF16)
    lo = (a - hi.astype(np.float32)).astype(BF16)
    return hi, lo


_MOD_TN = 512


def _mod_kernel(cb_ref, w_ref, b_ref, o_ref, s_scr):
    @pl.when((pl.program_id(0) == 0) & (pl.program_id(1) == 0))
    def _():
        c = cb_ref[...]
        s_scr[...] = c * jax.nn.sigmoid(c)

    o_ref[...] = jnp.zeros_like(o_ref)
    for n in range(_MOD_TN // LANES):
        sl = slice(n * LANES, (n + 1) * LANES)
        w = w_ref[:, sl]
        for j in range(N_COND):
            acc = jnp.sum(w * s_scr[j], axis=0, keepdims=True)
            o_ref[j:j + 1, sl] = acc + b_ref[:, sl]


def _modulation(cond, w_mod, b_mod):
    cb = jnp.broadcast_to(cond[:, :, None], (N_COND, D_MODEL, LANES))
    n6 = 6 * D_MODEL
    out = pl.pallas_call(
        _mod_kernel,
        grid=(DEPTH, n6 // _MOD_TN),
        in_specs=[
            pl.BlockSpec((N_COND, D_MODEL, LANES), lambda l, j: (0, 0, 0)),
            pl.BlockSpec((None, D_MODEL, _MOD_TN), lambda l, j: (l, 0, j)),
            pl.BlockSpec((None, 1, _MOD_TN), lambda l, j: (l, 0, j)),
        ],
        out_specs=pl.BlockSpec((None, COND_PAD, _MOD_TN), lambda l, j: (l, 0, j)),
        out_shape=jax.ShapeDtypeStruct((DEPTH, COND_PAD, n6), F32),
        scratch_shapes=[pltpu.VMEM((N_COND, D_MODEL, LANES), F32)],
        compiler_params=_params(2),
        name="adaln_mod",
    )(cb, w_mod, b_mod.reshape(DEPTH, 1, n6))
    out = out.reshape(DEPTH, COND_PAD, 6, D_MODEL).transpose(0, 2, 1, 3)
    return out.reshape(DEPTH, 6, COND_PAD, 1, D_MODEL)


def _mod_spec(layer, which, tm):
    return pl.BlockSpec((None, None, None, 1, D_MODEL),
                        lambda i, *_: (layer, which, _cond_of_row(i * tm), 0, 0))


_PRO_TM = 512


def _prologue_kernel(xp_ref, xs_ref, sc_ref, sh_ref, x_ref, xm_ref):
    i = pl.program_id(0)

    def emit(x):
        x_ref[...] = x
        xm_ref[...] = (x * (1.0 + sc_ref[...]) + sh_ref[...]).astype(BF16)

    @pl.when(i < NP // _PRO_TM)
    def _():
        emit(xp_ref[...])

    @pl.when(i >= NP // _PRO_TM)
    def _():
        emit(xs_ref[...])


def _prologue(xp, xs, mod):
    tm = _PRO_TM
    npt = NP // tm
    return pl.pallas_call(
        _prologue_kernel,
        grid=(NT // tm,),
        in_specs=[
            pl.BlockSpec((tm, D_MODEL), lambda i: (jnp.minimum(i, npt - 1), 0)),
            pl.BlockSpec((tm, D_MODEL), lambda i: (jnp.maximum(i - npt, 0), 0)),
            _mod_spec(0, 1, tm),
            _mod_spec(0, 0, tm),
        ],
        out_specs=[pl.BlockSpec((tm, D_MODEL), lambda i: (i, 0)),
                   pl.BlockSpec((tm, D_MODEL), lambda i: (i, 0))],
        out_shape=[jax.ShapeDtypeStruct((NT, D_MODEL), F32),
                   jax.ShapeDtypeStruct((NT, D_MODEL), BF16)],
        compiler_params=_params(1),
        name="prologue",
    )(xp, xs, mod, mod)


_INP_TM = 1024
_INP_TN = 768


def _inproj_kernel(x_ref, w_ref, o_ref):
    o_ref[...] = jnp.dot(x_ref[...], w_ref[...].astype(BF16), preferred_element_type=F32)


def _inproj(xm, w_in, layer):
    tm, tn = _INP_TM, _INP_TN
    return pl.pallas_call(
        _inproj_kernel,
        grid=(NT // tm, IN_W // tn),
        in_specs=[
            pl.BlockSpec((tm, D_MODEL), lambda i, j: (i, 0)),
            pl.BlockSpec((None, D_MODEL, tn), lambda i, j: (layer, 0, j)),
        ],
        out_specs=pl.BlockSpec((tm, tn), lambda i, j: (i, j)),
        out_shape=jax.ShapeDtypeStruct((NT, IN_W), F32),
        compiler_params=_params(2),
        name="inproj",
    )(xm, w_in)


def _attend(q, k, v, bias=None, valid=None, sink=None):
    s = lax.dot_general(q, k, (((1,), (1,)), ((), ())), preferred_element_type=F32)
    if bias is not None:
        s = s + bias
    if valid is not None:
        s = jnp.where(valid, s, NEG_INF)
    m = jnp.max(s, axis=-1, keepdims=True)
    if sink is not None:
        m = jnp.maximum(m, sink)
    p = jnp.exp(s - m)
    denom = jnp.sum(p, axis=-1, keepdims=True)
    if sink is not None:
        denom = denom + jnp.exp(sink - m)
    o = jnp.dot(p.astype(BF16), v, preferred_element_type=F32)
    return o / denom


def _head(x, h):
    return x[:, h * HEAD_DIM:(h + 1) * HEAD_DIM]


def _rope(x, cos, sin_signed):
    width = x.shape[1]
    lane = lax.broadcasted_iota(jnp.int32, x.shape, 1)
    first = (lane % 32) < 16
    partner = jnp.where(first, pltpu.roll(x, width - 16, 1), pltpu.roll(x, 16, 1))
    return x * cos + partner * sin_signed


def _rope_tables(n_heads):
    half = HEAD_DIM // 2
    inv = ROPE_THETA ** (-np.arange(0, half, 2, dtype=np.float64) / half)
    t = np.arange(DEC_SEQ)
    inv = inv.astype(np.float32).astype(np.float64)
    ang_r = (t // GRID_W)[:, None].astype(np.float64) * inv[None, :]
    ang_c = (t % GRID_W)[:, None].astype(np.float64) * inv[None, :]
    cos = np.concatenate([np.cos(ang_r), np.cos(ang_r), np.cos(ang_c), np.cos(ang_c)], -1)
    sin = np.concatenate([-np.sin(ang_r), np.sin(ang_r), -np.sin(ang_c), np.sin(ang_c)], -1)
    cos = np.tile(cos, (1, n_heads)).astype(np.float32)
    sin = np.tile(sin, (1, n_heads)).astype(np.float32)
    return cos, sin


def _ctx_attn_kernel(layer, sink_ref, qa_ref, ka_ref, va_ref, qc_ref, kvc_ref, oa_ref, oc_ref):
    qa = (qa_ref[...] * ATTN_SCALE).astype(BF16)
    ka = ka_ref[...].astype(BF16)
    va = va_ref[...].astype(BF16)
    outs = [_attend(_head(qa, h), _head(ka, h), _head(va, h)) for h in range(NA_HEADS)]
    oa_ref[...] = jnp.concatenate(outs, axis=-1).astype(BF16)

    qc = (qc_ref[...] * ATTN_SCALE).astype(BF16)
    kvc = kvc_ref[...].astype(BF16)
    kc, vc = kvc[:, :SWA_KV_W], kvc[:, SWA_KV_W:]
    outs = []
    for h in range(SWA_HEADS):
        hk = h // SWA_GROUP
        outs.append(_attend(_head(qc, h), _head(kc, hk), _head(vc, hk), sink=sink_ref[layer, h]))
    oc_ref[...] = jnp.concatenate(outs, axis=-1).astype(BF16)


def _ctx_attention(proj, swa_sink, layer):
    blk = lambda w, c: pl.BlockSpec((SEQ, w), lambda b: (b, c // w))
    return pl.pallas_call(
        functools.partial(_ctx_attn_kernel, layer),
        grid=(BATCH,),
        in_specs=[
            pl.BlockSpec(memory_space=pltpu.SMEM),
            blk(NA_W, COL_QA), blk(NA_W, COL_KA), blk(NA_W, COL_VA),
            blk(SWA_W, COL_QC), blk(2 * SWA_KV_W, COL_KC),
        ],
        out_specs=[pl.BlockSpec((SEQ, NA_W), lambda b: (b, 0)),
                   pl.BlockSpec((SEQ, SWA_W), lambda b: (b, 0))],
        out_shape=[jax.ShapeDtypeStruct((NP, NA_W), BF16),
                   jax.ShapeDtypeStruct((NP, SWA_W), BF16)],
        compiler_params=_params(1),
        name="ctx_attn",
    )(swa_sink, proj, proj, proj, proj, proj)


_NA_BAND = NA_WIN_ROWS * GRID_W


def _na_band_start(r):
    return jnp.clip(r - NA_WIN_ROWS // 2, 0, GRID_ROWS - NA_WIN_ROWS)


def _na_kernel(q_ref, k_ref, v_ref, ck_ref, cv_ref, bias_ref, o_ref):
    r = pl.program_id(1)
    start = pl.multiple_of(_na_band_start(r) * GRID_W, GRID_W)
    q = (q_ref[...] * ATTN_SCALE).astype(BF16)
    k = jnp.concatenate([k_ref[pl.ds(start, _NA_BAND), :].astype(BF16), ck_ref[...].astype(BF16)], axis=0)
    v = jnp.concatenate([v_ref[pl.ds(start, _NA_BAND), :].astype(BF16), cv_ref[...].astype(BF16)], axis=0)
    outs = [_attend(_head(q, h), _head(k, h), _head(v, h), bias=bias_ref[h]) for h in range(NA_HEADS)]
    o_ref[...] = jnp.concatenate(outs, axis=-1).astype(BF16)


def _na_bias_tables(rpb_l):
    nc = 2 * NA_WIN_COLS - 1
    col = np.arange(GRID_W)
    dc = np.clip(col[None, :] - col[:, None], -(NA_WIN_COLS - 1), NA_WIN_COLS - 1) + NA_WIN_COLS - 1
    onehot = (dc.reshape(-1)[None, :] == np.arange(nc)[:, None]).astype(np.float32)
    cs = np.clip(col - NA_WIN_COLS // 2, 0, GRID_W - NA_WIN_COLS)
    in_win = (col[None, :] >= cs[:, None]) & (col[None, :] < cs[:, None] + NA_WIN_COLS)
    toe = jnp.einsum('hrd,dn->hrn', rpb_l.astype(F32), jnp.asarray(onehot),
                     precision=lax.Precision.HIGHEST)
    toe = toe.reshape(NA_HEADS, 2 * NA_WIN_ROWS - 1, GRID_W, GRID_W)
    toe = jnp.where(jnp.asarray(in_win)[None, None], toe, NEG_INF)
    strips = jnp.stack([toe[:, o:o + NA_WIN_ROWS] for o in range(NA_WIN_ROWS)], 0)
    strips = strips.transpose(0, 1, 3, 2, 4).reshape(NA_WIN_ROWS, NA_HEADS, GRID_W, _NA_BAND)
    return jnp.pad(strips, ((0, 0), (0, 0), (0, 0), (0, PAST_LEN)))


def _na_latent(proj, cache_k, cache_v, rpb_l, layer):
    bias = _na_bias_tables(rpb_l)
    row0 = NP // GRID_W
    qmap = lambda b, r: (row0 + b * GRID_ROWS + r, 0)
    kvspec = lambda c: pl.BlockSpec((DEC_SEQ, NA_W), lambda b, r: (NP // DEC_SEQ + b, c // NA_W))
    cspec = pl.BlockSpec((None, None, PAST_LEN, NA_W), lambda b, r: (b, layer, 0, 0))

    def bias_map(b, r):
        return (_na_band_start(r) - r + NA_WIN_ROWS - 1, 0, 0, 0)

    return pl.pallas_call(
        _na_kernel,
        grid=(DEC_BATCH, GRID_ROWS),
        in_specs=[
            pl.BlockSpec((GRID_W, NA_W), qmap),
            kvspec(COL_KA), kvspec(COL_VA), cspec, cspec,
            pl.BlockSpec((None, NA_HEADS, GRID_W, _NA_BAND + PAST_LEN), bias_map),
        ],
        out_specs=pl.BlockSpec((GRID_W, NA_W), lambda b, r: (b * GRID_ROWS + r, 0)),
        out_shape=jax.ShapeDtypeStruct((NS, NA_W), BF16),
        compiler_params=_params(2),
        name="na_latent",
    )(proj, proj, proj, cache_k.reshape(DEC_BATCH, DEPTH, PAST_LEN, NA_W),
      cache_v.reshape(DEC_BATCH, DEPTH, PAST_LEN, NA_W), bias)


_SWA_KEYS = 3 * SWA_BLOCK


def _swa_kernel(layer, sink_ref, q_ref, kv_ref, ck_ref, cv_ref, cq_ref, sq_ref, ckt_ref, skt_ref,
                o_ref):
    n = pl.program_id(1)
    start = pl.multiple_of(jnp.clip((n - 1) * SWA_BLOCK, 0, DEC_SEQ - _SWA_KEYS), SWA_BLOCK)
    q = (_rope(q_ref[...], cq_ref[...], sq_ref[...]) * ATTN_SCALE).astype(BF16)
    kv = kv_ref[pl.ds(start, _SWA_KEYS), :]
    kw = _rope(kv[:, :SWA_KV_W], ckt_ref[pl.ds(start, _SWA_KEYS), :], skt_ref[pl.ds(start, _SWA_KEYS), :])
    k = jnp.concatenate([kw.astype(BF16), ck_ref[...].astype(BF16)], axis=0)
    v = jnp.concatenate([kv[:, SWA_KV_W:].astype(BF16), cv_ref[...].astype(BF16)], axis=0)
    nk = _SWA_KEYS + PAST_LEN
    qpos = n * SWA_BLOCK + lax.broadcasted_iota(jnp.int32, (SWA_BLOCK, nk), 0)
    kidx = lax.broadcasted_iota(jnp.int32, (SWA_BLOCK, nk), 1)
    valid = (jnp.abs(qpos - (start + kidx)) <= SWA_WINDOW) | (kidx >= _SWA_KEYS)
    outs = []
    for h in range(SWA_HEADS):
        hk = h // SWA_GROUP
        outs.append(_attend(_head(q, h), _head(k, hk), _head(v, hk), valid=valid, sink=sink_ref[layer, h]))
    o_ref[...] = jnp.concatenate(outs, axis=-1).astype(BF16)


def _swa_latent(proj, cache_k, cache_v, swa_sink, layer):
    cq, sq = _rope_tables(SWA_HEADS)
    ck, sk = _rope_tables(SWA_KV_HEADS)
    row0 = NP // SWA_BLOCK
    nblk = DEC_SEQ // SWA_BLOCK
    qmap = lambda b, n: (row0 + b * nblk + n, COL_QC // SWA_W)
    omap = lambda b, n: (b * nblk + n, 0)
    cspec = pl.BlockSpec((None, None, PAST_LEN, SWA_KV_W), lambda b, n: (b, layer, 0, 0))
    tq = pl.BlockSpec((SWA_BLOCK, SWA_W), lambda b, n: (n, 0))
    tk = pl.BlockSpec((DEC_SEQ, SWA_KV_W), lambda b, n: (0, 0))
    return pl.pallas_call(
        functools.partial(_swa_kernel, layer),
        grid=(DEC_BATCH, nblk),
        in_specs=[
            pl.BlockSpec(memory_space=pltpu.SMEM),
            pl.BlockSpec((SWA_BLOCK, SWA_W), qmap),
            pl.BlockSpec((DEC_SEQ, 2 * SWA_KV_W), lambda b, n: (NP // DEC_SEQ + b, COL_KC // (2 * SWA_KV_W))),
            cspec, cspec, tq, tq, tk, tk,
        ],
        out_specs=pl.BlockSpec((SWA_BLOCK, SWA_W), omap),
        out_shape=jax.ShapeDtypeStruct((NS, SWA_W), BF16),
        compiler_params=_params(2),
        name="swa_latent",
    )(swa_sink, proj, proj, cache_k.reshape(DEC_BATCH, DEPTH, PAST_LEN, SWA_KV_W),
      cache_v.reshape(DEC_BATCH, DEPTH, PAST_LEN, SWA_KV_W),
      jnp.asarray(cq), jnp.asarray(sq), jnp.asarray(ck), jnp.asarray(sk))


def _dft_mats():
    bk, n = HY_BLK, HY_NFFT
    s = np.arange(bk)
    ang = 2.0 * np.pi * ((s[:, None] * s[None, :]) % n) / n
    fwd = np.zeros((n, bk))
    fwd[:bk] = np.cos(ang)
    fwd[bk:] = -np.sin(ang)
    fwd[bk] = (-1.0) ** s
    inv = np.zeros((bk, n))
    inv[:, :bk] = (2.0 / n) * np.cos(ang)
    inv[:, 0] = 1.0 / n
    inv[:, bk:] = -(2.0 / n) * np.sin(ang)
    inv[:, bk] = (1.0 / n) * (-1.0) ** s
    fh, fl = _bf16_split_np(fwd)
    ih, il = _bf16_split_np(inv)
    return np.concatenate([fh, fh, fl], axis=1), np.concatenate([ih, ih, il], axis=1)


def _split3(x):
    hi = x.astype(BF16)
    lo = (x - hi.astype(F32)).astype(BF16)
    return jnp.concatenate([hi, lo, hi], axis=0)


def _hy_features(L):
    pos = np.arange(L, dtype=np.float64)
    t = (pos.astype(np.float32) / np.float32(max(L - 1, 1))).astype(np.float64)
    bands = np.linspace(1e-4, HY_POS_BANDS - 1, HY_POS_BANDS, dtype=np.float32).astype(np.float64)
    ang = np.float64(np.float32(2.0 * math.pi / L)) * pos[:, None] * bands[None, :]
    feat = np.concatenate([t[:, None], np.cos(ang), -np.sin(ang)], -1)
    feat = np.pad(feat, ((0, 0), (0, HY_FFN - HY_POS_DIM)))
    deltas = np.abs(np.linspace(math.log(HY_DECAY_TARGET) / HY_SLOW_DECAY,
                                math.log(HY_DECAY_TARGET) / HY_FAST_DECAY, HY_WIDTH, dtype=np.float32))
    window = np.exp(-t[:, None] * deltas[None, :].astype(np.float64)) + HY_MOD_SHIFT
    return feat.astype(np.float32), window.astype(np.float32)


_HYF_TC = 512


def _hy_filter_kernel(feat_ref, w1_ref, b1_ref, f1_ref, w2_ref, b2_ref, f2_ref, w3f_ref, w3b_ref, win_ref,
                      hf_ref, hb_ref):
    hp = lax.Precision.HIGHEST
    z = jnp.dot(feat_ref[...], w1_ref[...], precision=hp, preferred_element_type=F32) + b1_ref[...]
    h = jnp.sin(f1_ref[...] * z)
    z = jnp.dot(h, w2_ref[...], precision=hp, preferred_element_type=F32) + b2_ref[...]
    h = jnp.sin(f2_ref[...] * z)
    win = win_ref[...]
    af = jnp.dot(h, w3f_ref[...], precision=hp, preferred_element_type=F32) * win
    ab = jnp.dot(h, w3b_ref[...], precision=hp, preferred_element_type=F32) * win
    nrm = jnp.sum(jnp.abs(af), axis=0, keepdims=True) + jnp.sum(jnp.abs(ab), axis=0, keepdims=True)
    hf_ref[...] = af / nrm
    hb_ref[...] = ab / nrm


def _hy_filter(L, layer, hy_w1p, hy_b1, hy_freq1, hy_w2, hy_b2, hy_freq2, hy_w3):
    feat, window = _hy_features(L)
    tc = _HYF_TC
    ncc = HY_WIDTH // tc
    ow = HY_ORDER * HY_WIDTH
    small = lambda shape: pl.BlockSpec((None,) + shape, lambda o, c: (layer,) + (0,) * len(shape))
    w3spec = lambda d: pl.BlockSpec((None, HY_FFN, tc), lambda o, c: (layer, 0, (d * HY_ORDER + o) * ncc + c))
    ospec = pl.BlockSpec((L, tc), lambda o, c: (0, o * ncc + c))
    vec = lambda a: a.reshape(DEPTH, 1, HY_FFN)
    return pl.pallas_call(
        _hy_filter_kernel,
        grid=(HY_ORDER, ncc),
        in_specs=[
            pl.BlockSpec((L, HY_FFN), lambda o, c: (0, 0)),
            small((HY_FFN, HY_FFN)), small((1, HY_FFN)), small((1, HY_FFN)),
            small((HY_FFN, HY_FFN)), small((1, HY_FFN)), small((1, HY_FFN)),
            w3spec(0), w3spec(1),
            pl.BlockSpec((L, tc), lambda o, c: (0, c)),
        ],
        out_specs=[ospec, ospec],
        out_shape=[jax.ShapeDtypeStruct((L, ow), F32), jax.ShapeDtypeStruct((L, ow), F32)],
        compiler_params=_params(2),
        name=f"hy_filter_{L}",
    )(jnp.asarray(feat), hy_w1p, vec(hy_b1), vec(hy_freq1), hy_w2, vec(hy_b2), vec(hy_freq2),
      hy_w3, hy_w3, jnp.asarray(window))


_HYS_TC = 256


def _hy_spectrum_kernel(nb, hf_ref, hb_ref, f3_ref, g_ref):
    n, bk = HY_NFFT, HY_BLK
    row = lax.broadcasted_iota(jnp.int32, (n, 1), 0)
    ones_lo = (row <= bk).astype(F32)
    sgn = jnp.where(row % 2 == 0, 1.0, -1.0).astype(F32)
    conj = jnp.where(row > bk, -1.0, 1.0).astype(F32)
    f3 = f3_ref[...]

    def spectra(h_ref, drop_lag0):
        first, tail = [], []
        for b in range(nb):
            blk = h_ref[b * bk:(b + 1) * bk, :]
            t = jnp.dot(f3, _split3(blk), preferred_element_type=F32)
            head = ones_lo * blk[0:1, :]
            if b == 0 and drop_lag0:
                t = t - head
                tail.append(sgn * t)
            else:
                tail.append(sgn * (t - head))
            first.append(t)
        return [first[d] + (tail[d - 1] if d >= 1 else 0.0) for d in range(nb)]

    gf = spectra(hf_ref, False)
    gb = spectra(hb_ref, True)
    g_ref[nb - 1] = gf[0] + conj * gb[0]
    for d in range(1, nb):
        g_ref[nb - 1 + d] = gf[d]
        g_ref[nb - 1 - d] = conj * gb[d]


def _hy_spectrum(hf, hb, f3):
    L = hf.shape[0]
    nb = L // HY_BLK
    ow = HY_ORDER * HY_WIDTH
    tc = _HYS_TC
    return pl.pallas_call(
        functools.partial(_hy_spectrum_kernel, nb),
        grid=(ow // tc,),
        in_specs=[
            pl.BlockSpec((L, tc), lambda c: (0, c)),
            pl.BlockSpec((L, tc), lambda c: (0, c)),
            pl.BlockSpec((HY_NFFT, 3 * HY_BLK), lambda c: (0, 0)),
        ],
        out_specs=pl.BlockSpec((2 * nb - 1, HY_NFFT, tc), lambda c: (0, 0, c)),
        out_shape=jax.ShapeDtypeStruct((2 * nb - 1, HY_NFFT, ow), F32),
        compiler_params=_params(1),
        name=f"hy_spectrum_{L}",
    )(hf, hb, f3)


def _spec_mul(x, g):
    bk = HY_BLK
    xa, xb = x[:bk], x[bk:]
    ga, gb = g[:bk], g[bk:]
    row0 = lax.broadcasted_iota(jnp.int32, xa.shape, 0) == 0
    bb = xb * gb
    pa = xa * ga - jnp.where(row0, 0.0, bb)
    pb = jnp.where(row0, bb, xa * gb + xb * ga)
    return jnp.concatenate([pa, pb], axis=0)


def _short_conv(u, w, b):
    L = u.shape[0]
    row = lax.broadcasted_iota(jnp.int32, u.shape, 0)
    prev = jnp.where(row == 0, 0.0, pltpu.roll(u, 1, 0))
    nxt = jnp.where(row == L - 1, 0.0, pltpu.roll(u, L - 1, 0))
    return prev * w[0:1] + u * w[1:2] + nxt * w[2:3] + b


def _hy_conv_kernel(nb, uv_ref, u1_ref, u2_ref, wv_ref, w1_ref, w2_ref, bv_ref, b1_ref, b2_ref, skip_ref,
                    g0_ref, g1_ref, f3_ref, i3_ref, o_ref):
    bk = HY_BLK
    v = _short_conv(uv_ref[...], wv_ref[...], bv_ref[...])
    gates = (_short_conv(u1_ref[...], w1_ref[...], b1_ref[...]),
             _short_conv(u2_ref[...], w2_ref[...], b2_ref[...]))
    cc = v.shape[1]
    f3 = f3_ref[...]
    i3 = i3_ref[...]
    z = v
    for o, g_ref in enumerate((g0_ref, g1_ref)):
        zcat = jnp.concatenate([z[j * bk:(j + 1) * bk] for j in range(nb)], axis=1)
        x = jnp.dot(f3, _split3(zcat), preferred_element_type=F32)
        prods = []
        for i in range(nb):
            acc = None
            for j in range(nb):
                t = _spec_mul(x[:, j * cc:(j + 1) * cc], g_ref[i - j + nb - 1])
                acc = t if acc is None else acc + t
            prods.append(acc)
        pcat = jnp.concatenate(prods, axis=1)
        y = jnp.dot(i3, _split3(pcat), preferred_element_type=F32)
        y = jnp.concatenate([y[:, i * cc:(i + 1) * cc] for i in range(nb)], axis=0)
        z = gates[o] * (y + skip_ref[o:o + 1, :] * z)
    o_ref[...] = z.astype(BF16)


def _hy_conv(proj, g, f3, i3, short_w, short_b, skip, layer, L, row0, nbatch, cc):
    nb = L // HY_BLK
    ncc = HY_WIDTH // cc
    rb0 = row0 // L
    ucol = lambda part: pl.BlockSpec((L, cc), lambda c, b: (rb0 + b, (COL_HY + part * HY_WIDTH) // cc + c))
    wcol = lambda part: pl.BlockSpec((None, HY_SHORT, cc), lambda c, b: (layer, 0, part * ncc + c))
    bcol = lambda part: pl.BlockSpec((None, 1, cc), lambda c, b: (layer, 0, part * ncc + c))
    gspec = lambda o: pl.BlockSpec((2 * nb - 1, HY_NFFT, cc), lambda c, b: (0, 0, o * ncc + c))
    in_specs = [
        ucol(0), ucol(1), ucol(2), wcol(0), wcol(1), wcol(2), bcol(0), bcol(1), bcol(2),
        pl.BlockSpec((None, HY_ORDER, cc), lambda c, b: (layer, 0, c)),
        gspec(0), gspec(1),
        pl.BlockSpec((HY_NFFT, 3 * HY_BLK), lambda c, b: (0, 0)),
        pl.BlockSpec((HY_BLK, 3 * HY_NFFT), lambda c, b: (0, 0)),
    ]
    args = [proj, proj, proj, short_w, short_w, short_w] + [short_b.reshape(DEPTH, 1, 3 * HY_WIDTH)] * 3 + [
        skip, g, g, f3, i3]
    return pl.pallas_call(
        functools.partial(_hy_conv_kernel, nb),
        grid=(ncc, nbatch),
        in_specs=in_specs,
        out_specs=pl.BlockSpec((L, cc), lambda c, b: (b, c)),
        out_shape=jax.ShapeDtypeStruct((nbatch * L, HY_WIDTH), BF16),
        compiler_params=_params(2),
        name=f"hy_conv_{L}",
    )(*args)


_OUT_TM = 512
_OUT_TK = 512


def _layer_norm(x, g, b):
    mu = jnp.mean(x, axis=-1, keepdims=True)
    xc = x - mu
    var = jnp.mean(xc * xc, axis=-1, keepdims=True)
    return xc * lax.rsqrt(var + LN_EPS) * g + b


def _outproj_kernel(oa_c, ob_c, oc_c, oa_l, ob_l, oc_l, w_ref, x_ref, g1_ref, lg_ref, lb_ref, sc_ref, sh_ref,
                    x1_ref, h_ref, acc_ref):
    i, k = pl.program_id(0), pl.program_id(1)
    w = w_ref[...].astype(BF16)
    is_ctx = i < NP // _OUT_TM

    def mixer(c_ref, l_ref):
        return jnp.where(is_ctx, c_ref[...], l_ref[...])

    @pl.when(k == 0)
    def _():
        acc_ref[...] = jnp.dot(mixer(oa_c, oa_l), w, preferred_element_type=F32)

    @pl.when((k == 1) | (k == 2))
    def _():
        acc_ref[...] += jnp.dot(mixer(ob_c, ob_l), w, preferred_element_type=F32)

    @pl.when(k == 3)
    def _():
        mix = acc_ref[...] + jnp.dot(mixer(oc_c, oc_l), w, preferred_element_type=F32)
        x1 = _layer_norm(DEEPNORM_ALPHA * x_ref[...] + g1_ref[...] * mix, lg_ref[...], lb_ref[...])
        x1_ref[...] = x1
        h_ref[...] = (x1 * (1.0 + sc_ref[...]) + sh_ref[...]).astype(h_ref.dtype)


def _outproj(ctx_mix, lat_mix, w_out, x, mod, ln_g, ln_b, layer, h_dtype):
    tm, tk = _OUT_TM, _OUT_TK
    npt = NP // tm
    vec = pl.BlockSpec((None, 1, D_MODEL), lambda i, k: (layer, 0, 0))
    crow = lambda i: jnp.minimum(i, npt - 1)
    lrow = lambda i: jnp.maximum(i - npt, 0)
    kb = lambda k: jnp.clip(k - 1, 0, 1)
    return pl.pallas_call(
        _outproj_kernel,
        grid=(NT // tm, MIX_W // tk),
        in_specs=[
            pl.BlockSpec((tm, tk), lambda i, k: (crow(i), 0)),
            pl.BlockSpec((tm, tk), lambda i, k: (crow(i), kb(k))),
            pl.BlockSpec((tm, tk), lambda i, k: (crow(i), 0)),
            pl.BlockSpec((tm, tk), lambda i, k: (lrow(i), 0)),
            pl.BlockSpec((tm, tk), lambda i, k: (lrow(i), kb(k))),
            pl.BlockSpec((tm, tk), lambda i, k: (lrow(i), 0)),
            pl.BlockSpec((None, tk, D_MODEL), lambda i, k: (layer, k, 0)),
            pl.BlockSpec((tm, D_MODEL), lambda i, k: (i, 0)),
            _mod_spec(layer, 2, tm), vec, vec, _mod_spec(layer, 4, tm), _mod_spec(layer, 3, tm),
        ],
        out_specs=[pl.BlockSpec((tm, D_MODEL), lambda i, k: (i, 0)),
                   pl.BlockSpec((tm, D_MODEL), lambda i, k: (i, 0))],
        out_shape=[jax.ShapeDtypeStruct((NT, D_MODEL), F32), jax.ShapeDtypeStruct((NT, D_MODEL), h_dtype)],
        scratch_shapes=[pltpu.VMEM((tm, D_MODEL), F32)],
        compiler_params=_params(2),
        name="outproj_ln",
    )(*ctx_mix, *lat_mix, w_out, x, mod, ln_g.reshape(DEPTH, 1, D_MODEL), ln_b.reshape(DEPTH, 1, D_MODEL),
      mod, mod)


_FFN_TM = 1024
_FFN_TF = 256


def _swiglu_partial(h, wg_ref, wu_ref, wd_ref):
    g = jnp.dot(h, wg_ref[...].astype(BF16), preferred_element_type=F32)
    u = jnp.dot(h, wu_ref[...].astype(BF16), preferred_element_type=F32)
    a = (g * jax.nn.sigmoid(g) * u).astype(BF16)
    return jnp.dot(a, wd_ref[...].astype(BF16), preferred_element_type=F32)


def _ffn_kernel(h_ref, wg_ref, wu_ref, wd_ref, y_ref):
    @pl.when(pl.program_id(1) == 0)
    def _():
        y_ref[...] = jnp.zeros_like(y_ref)

    y_ref[...] += _swiglu_partial(h_ref[...], wg_ref, wu_ref, wd_ref)


def _ffn_dense(h, wg, wu, wd, idx):
    tm, tf = _FFN_TM, _FFN_TF
    return pl.pallas_call(
        _ffn_kernel,
        grid=(NT // tm, D_FF // tf),
        in_specs=[
            pl.BlockSpec((tm, D_MODEL), lambda i, f: (i, 0)),
            pl.BlockSpec((None, D_MODEL, tf), lambda i, f: (idx, 0, f)),
            pl.BlockSpec((None, D_MODEL, tf), lambda i, f: (idx, 0, f)),
            pl.BlockSpec((None, tf, D_MODEL), lambda i, f: (idx, f, 0)),
        ],
        out_specs=pl.BlockSpec((tm, D_MODEL), lambda i, f: (i, 0)),
        out_shape=jax.ShapeDtypeStruct((NT, D_MODEL), F32),
        compiler_params=_params(2),
        name="ffn_dense",
    )(h, wg, wu, wd)


_RT_TM = 1024


def _router_kernel(h_ref, rt_ref, r_ref):
    logits = lax.dot_general(rt_ref[...].astype(BF16), h_ref[...].astype(BF16), (((1,), (1,)), ((), ())),
                             preferred_element_type=F32)
    idx = lax.broadcasted_iota(jnp.int32, logits.shape, 0)
    m1 = jnp.max(logits, axis=0, keepdims=True)
    i1 = jnp.min(jnp.where(logits == m1, idx, N_EXPERTS), axis=0, keepdims=True)
    rest = jnp.where(idx == i1, -jnp.inf, logits)
    m2 = jnp.max(rest, axis=0, keepdims=True)
    i2 = jnp.min(jnp.where(rest == m2, idx, N_EXPERTS), axis=0, keepdims=True)
    e2 = jnp.exp(m2 - m1)
    g1 = 1.0 / (1.0 + e2)
    g2 = e2 / (1.0 + e2)
    r_ref[...] = (jnp.where(idx == 0, i1.astype(F32), 0.0) + jnp.where(idx == 1, i2.astype(F32), 0.0)
                  + jnp.where(idx == 2, g1, 0.0) + jnp.where(idx == 3, g2, 0.0))


def _router(h, router_t):
    tm = _RT_TM
    return pl.pallas_call(
        _router_kernel,
        grid=(NT // tm,),
        in_specs=[pl.BlockSpec((tm, D_MODEL), lambda i: (i, 0)),
                  pl.BlockSpec((N_EXPERTS, D_MODEL), lambda i: (0, 0))],
        out_specs=pl.BlockSpec((N_EXPERTS, tm), lambda i: (0, i)),
        out_shape=jax.ShapeDtypeStruct((N_EXPERTS, NT), F32),
        compiler_params=_params(1),
        name="moe_router",
    )(h, router_t)


_MOE_TM = 1152
_MOE_CHUNK = 384
_MOE_TF = 256
_MOE_TILES = (TOP_K * NT + N_EXPERTS * (_MOE_TM - 1)) // _MOE_TM
_MOE_ROWS = _MOE_TILES * _MOE_TM


def _moe_routed_kernel(te_ref, nv_ref, src_ref, dst_ref, h_hbm, gate_ref, wg_ref, wu_ref, wd_ref,
                       y_hbm, xs, xb, acc, stage, gsem, ssem):
    del te_ref
    i, f = pl.program_id(0), pl.program_id(1)
    tm = _MOE_TM
    nv = nv_ref[i]

    def gather_copy(tile, slot, r):
        return pltpu.make_async_copy(h_hbm.at[pl.ds(src_ref[tile * tm + r], 1), :],
                                     xs.at[slot, pl.ds(r, 1), :], gsem.at[slot])

    def scatter_copy(tile, r):
        return pltpu.make_async_copy(stage.at[pl.ds(r, 1), :],
                                     y_hbm.at[pl.ds(dst_ref[tile * tm + r], 1), :], ssem.at[0])

    def for_rows(tile, fn):
        def body(r, carry):
            fn(r)
            return carry
        lax.fori_loop(0, nv_ref[tile], body, 0)

    @pl.when(f == 0)
    def _():
        @pl.when(i == 0)
        def _():
            xs[...] = jnp.zeros_like(xs)
            for_rows(0, lambda r: gather_copy(0, 0, r).start())

        slot = i % 2
        for_rows(i, lambda r: gather_copy(i, slot, r).wait())

        @pl.when(i + 1 < _MOE_TILES)
        def _():
            for_rows(i + 1, lambda r: gather_copy(i + 1, 1 - slot, r).start())

        xb[...] = xs[slot].astype(BF16)
        acc[...] = jnp.zeros_like(acc)

    wg = wg_ref[...].astype(BF16)
    wu = wu_ref[...].astype(BF16)
    wd = wd_ref[...].astype(BF16)
    for c in range(tm // _MOE_CHUNK):
        rows = slice(c * _MOE_CHUNK, (c + 1) * _MOE_CHUNK)

        @pl.when(c * _MOE_CHUNK < nv)
        def _():
            h = xb[rows, :]
            g = jnp.dot(h, wg, preferred_element_type=F32)
            u = jnp.dot(h, wu, preferred_element_type=F32)
            a = (g * jax.nn.sigmoid(g) * u).astype(BF16)
            acc[rows, :] += jnp.dot(a, wd, preferred_element_type=F32)

    @pl.when(f == pl.num_programs(1) - 1)
    def _():
        @pl.when(i > 0)
        def _():
            for_rows(i - 1, lambda r: scatter_copy(i - 1, r).wait())

        stage[...] = acc[...] * gate_ref[...]
        for_rows(i, lambda r: scatter_copy(i, r).start())

        @pl.when(i == _MOE_TILES - 1)
        def _():
            for_rows(i, lambda r: scatter_copy(i, r).wait())


def _moe(h, router, wg, wu, wd, idx):
    tm, tf = _MOE_TM, _MOE_TF
    nf = D_FF_EXPERT // tf
    r = _router(h, router[idx].T)
    pairs = TOP_K * NT
    e_flat = r[0:TOP_K].astype(jnp.int32).reshape(pairs)
    gate_flat = r[TOP_K:2 * TOP_K].reshape(pairs)
    onehot = (e_flat[:, None] == jnp.arange(N_EXPERTS, dtype=jnp.int32)[None, :]).astype(jnp.int32)
    csum = jnp.cumsum(onehot, axis=0)
    rank = jnp.take_along_axis(csum, e_flat[:, None], axis=1)[:, 0] - 1
    counts = csum[-1]
    tiles_e = (counts + tm - 1) // tm
    tile_end = jnp.cumsum(tiles_e)
    tile_start = tile_end - tiles_e
    pos = tile_start[e_flat] * tm + rank
    pair_id = jnp.arange(pairs, dtype=jnp.int32)
    dst = jnp.zeros((_MOE_ROWS,), jnp.int32).at[pos].set(pair_id, unique_indices=True)
    src = dst % NT
    gates = gate_flat[dst].reshape(_MOE_ROWS, 1)
    tile = jnp.arange(_MOE_TILES, dtype=jnp.int32)
    te = jnp.minimum(jnp.sum((tile[:, None] >= tile_end[None, :]).astype(jnp.int32), axis=1), N_EXPERTS - 1)
    nv = jnp.clip(counts[te] - (tile - tile_start[te]) * tm, 0, tm).astype(jnp.int32)

    def wmap(i, f, te_ref, nv_ref, *_):
        return (idx, te_ref[i], 0, jnp.where(nv_ref[i] > 0, f, nf - 1))

    def wdmap(i, f, te_ref, nv_ref, *_):
        return (idx, te_ref[i], jnp.where(nv_ref[i] > 0, f, nf - 1), 0)

    grid_spec = pltpu.PrefetchScalarGridSpec(
        num_scalar_prefetch=4,
        grid=(_MOE_TILES, nf),
        in_specs=[
            pl.BlockSpec(memory_space=pl.ANY),
            pl.BlockSpec((tm, 1), lambda i, f, *_: (i, 0)),
            pl.BlockSpec((None, None, D_MODEL, tf), wmap),
            pl.BlockSpec((None, None, D_MODEL, tf), wmap),
            pl.BlockSpec((None, None, tf, D_MODEL), wdmap),
        ],
        out_specs=pl.BlockSpec(memory_space=pl.ANY),
        scratch_shapes=[
            pltpu.VMEM((2, tm, D_MODEL), F32),
            pltpu.VMEM((tm, D_MODEL), BF16),
            pltpu.VMEM((tm, D_MODEL), F32),
            pltpu.VMEM((tm, D_MODEL), F32),
            pltpu.SemaphoreType.DMA((2,)),
            pltpu.SemaphoreType.DMA((1,)),
        ],
    )
    return pl.pallas_call(
        _moe_routed_kernel,
        grid_spec=grid_spec,
        out_shape=jax.ShapeDtypeStruct((pairs, D_MODEL), F32),
        compiler_params=_params(2, V7X_VMEM_BYTES * 15 // 16),
        name="moe_routed",
    )(te, nv, src, dst, h, gates, wg, wu, wd)


_EPI_TM = 512


def _epilogue_kernel(n_parts, x_ref, *refs):
    y_refs = refs[:n_parts]
    g2_ref, lg_ref, lb_ref, sc_ref, sh_ref, x2_ref, xm_ref = refs[n_parts:]
    y = y_refs[0][...]
    for y_ref in y_refs[1:]:
        y = y + y_ref[...]
    x2 = _layer_norm(DEEPNORM_ALPHA * x_ref[...] + g2_ref[...] * y, lg_ref[...], lb_ref[...])
    x2_ref[...] = x2
    xm_ref[...] = (x2 * (1.0 + sc_ref[...]) + sh_ref[...]).astype(BF16)


def _epilogue(x1, y, mod, ln_g, ln_b, layer):
    tm = _EPI_TM
    n_parts = y.shape[0] // NT
    nxt = min(layer + 1, DEPTH - 1)
    vec = pl.BlockSpec((None, 1, D_MODEL), lambda i: (layer, 0, 0))
    row = pl.BlockSpec((tm, D_MODEL), lambda i: (i, 0))
    parts = [pl.BlockSpec((tm, D_MODEL), functools.partial(lambda p, i: (p * (NT // tm) + i, 0), p))
             for p in range(n_parts)]
    return pl.pallas_call(
        functools.partial(_epilogue_kernel, n_parts),
        grid=(NT // tm,),
        in_specs=[row] + parts + [_mod_spec(layer, 5, tm), vec, vec, _mod_spec(nxt, 1, tm),
                                  _mod_spec(nxt, 0, tm)],
        out_specs=[row, row],
        out_shape=[jax.ShapeDtypeStruct((NT, D_MODEL), F32), jax.ShapeDtypeStruct((NT, D_MODEL), BF16)],
        compiler_params=_params(1),
        name="ffn_epilogue",
    )(x1, *([y] * n_parts), mod, ln_g.reshape(DEPTH, 1, D_MODEL), ln_b.reshape(DEPTH, 1, D_MODEL), mod, mod)


def kernel(x_prompt, x_sample, cache_na_k, cache_na_v, cache_swa_k, cache_swa_v, c, c_ctx, w_mod, b_mod, w_in, w_out, na_rpb, swa_sink, hy_short_w, hy_short_b, hy_w1, hy_b1, hy_freq1, hy_w2, hy_b2, hy_freq2, hy_w3, hy_skip, ln1_g, ln1_b, ln2_g, ln2_b, ffn_w_gate, ffn_w_up, ffn_w_down, moe_router, moe_w_gate, moe_w_up, moe_w_down):
    cond = jnp.concatenate([c_ctx[None, :], c], axis=0)
    mod = _modulation(cond, w_mod, b_mod)
    x, xm = _prologue(x_prompt.reshape(NP, D_MODEL), x_sample.reshape(NS, D_MODEL), mod)

    f3_np, i3_np = _dft_mats()
    f3, i3 = jnp.asarray(f3_np), jnp.asarray(i3_np)
    hy_w1p = jnp.pad(hy_w1, ((0, 0), (0, HY_FFN - HY_POS_DIM), (0, 0)))

    kv_out = []
    for l in range(DEPTH):
        proj = _inproj(xm, w_in, l)
        kv_out.append(proj[:NP])

        oa_c, oc_c = _ctx_attention(proj, swa_sink, l)
        oa_l = _na_latent(proj, cache_na_k, cache_na_v, na_rpb[l], l)
        oc_l = _swa_latent(proj, cache_swa_k, cache_swa_v, swa_sink, l)

        filt = (l, hy_w1p, hy_b1, hy_freq1, hy_w2, hy_b2, hy_freq2, hy_w3)
        g_ctx = _hy_spectrum(*_hy_filter(SEQ, *filt), f3)
        g_lat = _hy_spectrum(*_hy_filter(DEC_SEQ, *filt), f3)
        ob_c = _hy_conv(proj, g_ctx, f3, i3, hy_short_w, hy_short_b, hy_skip, l, SEQ, 0, BATCH, 512)
        ob_l = _hy_conv(proj, g_lat, f3, i3, hy_short_w, hy_short_b, hy_skip, l, DEC_SEQ, NP, DEC_BATCH, 128)

        x1, h = _outproj((oa_c, ob_c, oc_c), (oa_l, ob_l, oc_l), w_out, x, mod, ln1_g, ln1_b, l,
                         BF16 if l % 2 == 0 else F32)
        if l % 2 == 0:
            y = _ffn_dense(h, ffn_w_gate, ffn_w_up, ffn_w_down, l // 2)
        else:
            y = _moe(h, moe_router, moe_w_gate, moe_w_up, moe_w_down, l // 2)
        x, xm = _epilogue(x1, y, mod, ln2_g, ln2_b, l)

    def cache(col, heads):
        parts = [p[:, col:col + heads * HEAD_DIM].reshape(BATCH, SEQ, heads, HEAD_DIM) for p in kv_out]
        return jnp.stack(parts, axis=1)

    return (x[:NP].reshape(BATCH, SEQ, D_MODEL), x[NP:].reshape(DEC_BATCH, DEC_SEQ, D_MODEL),
            cache(COL_KA, NA_HEADS), cache(COL_VA, NA_HEADS),
            cache(COL_KC, SWA_KV_HEADS), cache(COL_VC, SWA_KV_HEADS))
```

```python
import functools
import math

import numpy as np
import jax
import jax.numpy as jnp
from jax import lax
from jax.experimental import pallas as pl
from jax.experimental.pallas import tpu as pltpu

F32 = jnp.float32
BF16 = jnp.bfloat16

D_MODEL = 2048
BATCH = 32
SEQ = 256
DEPTH = 2
DEC_BATCH = 2
DEC_SEQ = 2048
PAST_LEN = 256
GRID_W = 64
GRID_ROWS = DEC_SEQ // GRID_W
HEAD_DIM = 64
ATTN_SCALE = HEAD_DIM ** -0.5
NA_HEADS = 8
NA_WIN_ROWS = 8
NA_WIN_COLS = 16
HY_WIDTH = 1024
HY_ORDER = 2
HY_SHORT = 3
HY_POS_BANDS = 16
HY_POS_DIM = 1 + 2 * HY_POS_BANDS
HY_FFN = 64
HY_FAST_DECAY = 0.3
HY_SLOW_DECAY = 1.5
HY_DECAY_TARGET = 1e-2
HY_MOD_SHIFT = 0.05
SWA_HEADS = 8
SWA_KV_HEADS = 2
SWA_GROUP = SWA_HEADS // SWA_KV_HEADS
SWA_WINDOW = 128
SWA_BLOCK = 128
NA_W = NA_HEADS * HEAD_DIM
SWA_W = SWA_HEADS * HEAD_DIM
SWA_KV_W = SWA_KV_HEADS * HEAD_DIM
MIX_W = NA_W + HY_WIDTH + SWA_W
IN_W = 3 * NA_W + 3 * HY_WIDTH + SWA_W + 2 * SWA_KV_W
D_FF = 7168
N_EXPERTS = 8
TOP_K = 2
D_FF_EXPERT = 7168
ROPE_THETA = 10000.0
LN_EPS = 1e-5
NEG_INF = -1e30
DEEPNORM_ALPHA = (2 * DEPTH) ** 0.25

NP = BATCH * SEQ
NS = DEC_BATCH * DEC_SEQ
NT = NP + NS
N_COND = 1 + DEC_BATCH
COND_PAD = 8

COL_QA, COL_KA, COL_VA = 0, NA_W, 2 * NA_W
COL_HY = 3 * NA_W
COL_QC = COL_HY + 3 * HY_WIDTH
COL_KC = COL_QC + SWA_W
COL_VC = COL_KC + SWA_KV_W

HY_BLK = 256
HY_NFFT = 2 * HY_BLK

LANES = 128
V7X_VMEM_BYTES = 64 * 1024 * 1024
VMEM_LIMIT_BYTES = V7X_VMEM_BYTES * 7 // 8


def _params(grid_rank):
    return pltpu.CompilerParams(dimension_semantics=("arbitrary",) * grid_rank,
                                vmem_limit_bytes=VMEM_LIMIT_BYTES)


def _cond_of_row(row):
    return jnp.maximum((row - NP) // DEC_SEQ + 1, 0)


def _bf16_split_np(a):
    a = np.asarray(a, np.float32)
    hi = a.astype(BF16)
    lo = (a - hi.astype(np.float32)).astype(BF16)
    return hi, lo


_MOD_TN = 512


def _mod_kernel(cb_ref, w_ref, b_ref, o_ref, s_scr):
    @pl.when((pl.program_id(0) == 0) & (pl.program_id(1) == 0))
    def _():
        c = cb_ref[...]
        s_scr[...] = c * jax.nn.sigmoid(c)

    o_ref[...] = jnp.zeros_like(o_ref)
    for n in range(_MOD_TN // LANES):
        sl = slice(n * LANES, (n + 1) * LANES)
        w = w_ref[:, sl]
        for j in range(N_COND):
            acc = jnp.sum(w * s_scr[j], axis=0, keepdims=True)
            o_ref[j:j + 1, sl] = acc + b_ref[:, sl]


def _modulation(cond, w_mod, b_mod):
    cb = jnp.broadcast_to(cond[:, :, None], (N_COND, D_MODEL, LANES))
    n6 = 6 * D_MODEL
    out = pl.pallas_call(
        _mod_kernel,
        grid=(DEPTH, n6 // _MOD_TN),
        in_specs=[
            pl.BlockSpec((N_COND, D_MODEL, LANES), lambda l, j: (0, 0, 0)),
            pl.BlockSpec((None, D_MODEL, _MOD_TN), lambda l, j: (l, 0, j)),
            pl.BlockSpec((None, 1, _MOD_TN), lambda l, j: (l, 0, j)),
        ],
        out_specs=pl.BlockSpec((None, COND_PAD, _MOD_TN), lambda l, j: (l, 0, j)),
        out_shape=jax.ShapeDtypeStruct((DEPTH, COND_PAD, n6), F32),
        scratch_shapes=[pltpu.VMEM((N_COND, D_MODEL, LANES), F32)],
        compiler_params=_params(2),
        name="adaln_mod",
    )(cb, w_mod, b_mod.reshape(DEPTH, 1, n6))
    out = out.reshape(DEPTH, COND_PAD, 6, D_MODEL).transpose(0, 2, 1, 3)
    return out.reshape(DEPTH, 6, COND_PAD, 1, D_MODEL)


def _mod_spec(layer, which, tm):
    return pl.BlockSpec((None, None, None, 1, D_MODEL),
                        lambda i, *_: (layer, which, _cond_of_row(i * tm), 0, 0))


_PRO_TM = 512


def _prologue_kernel(xp_ref, xs_ref, sc_ref, sh_ref, x_ref, xm_ref):
    i = pl.program_id(0)

    def emit(x):
        x_ref[...] = x
        xm_ref[...] = (x * (1.0 + sc_ref[...]) + sh_ref[...]).astype(BF16)

    @pl.when(i < NP // _PRO_TM)
    def _():
        emit(xp_ref[...])

    @pl.when(i >= NP // _PRO_TM)
    def _():
        emit(xs_ref[...])


def _prologue(xp, xs, mod):
    tm = _PRO_TM
    npt = NP // tm
    return pl.pallas_call(
        _prologue_kernel,
        grid=(NT // tm,),
        in_specs=[
            pl.BlockSpec((tm, D_MODEL), lambda i: (jnp.minimum(i, npt - 1), 0)),
            pl.BlockSpec((tm, D_MODEL), lambda i: (jnp.maximum(i - npt, 0), 0)),
            _mod_spec(0, 1, tm),
            _mod_spec(0, 0, tm),
        ],
        out_specs=[pl.BlockSpec((tm, D_MODEL), lambda i: (i, 0)),
                   pl.BlockSpec((tm, D_MODEL), lambda i: (i, 0))],
        out_shape=[jax.ShapeDtypeStruct((NT, D_MODEL), F32),
                   jax.ShapeDtypeStruct((NT, D_MODEL), BF16)],
        compiler_params=_params(1),
        name="prologue",
    )(xp, xs, mod, mod)


_INP_TM = 1024
_INP_TN = 768


def _inproj_kernel(x_ref, w_ref, o_ref):
    o_ref[...] = jnp.dot(x_ref[...], w_ref[...].astype(BF16), preferred_element_type=F32)


def _inproj(xm, w_in, layer):
    tm, tn = _INP_TM, _INP_TN
    return pl.pallas_call(
        _inproj_kernel,
        grid=(NT // tm, IN_W // tn),
        in_specs=[
            pl.BlockSpec((tm, D_MODEL), lambda i, j: (i, 0)),
            pl.BlockSpec((None, D_MODEL, tn), lambda i, j: (layer, 0, j)),
        ],
        out_specs=pl.BlockSpec((tm, tn), lambda i, j: (i, j)),
        out_shape=jax.ShapeDtypeStruct((NT, IN_W), F32),
        compiler_params=_params(2),
        name="inproj",
    )(xm, w_in)


def _attend(q, k, v, bias=None, valid=None, sink=None):
    s = lax.dot_general(q, k, (((1,), (1,)), ((), ())), preferred_element_type=F32)
    if bias is not None:
        s = s + bias
    if valid is not None:
        s = jnp.where(valid, s, NEG_INF)
    m = jnp.max(s, axis=-1, keepdims=True)
    if sink is not None:
        m = jnp.maximum(m, sink)
    p = jnp.exp(s - m)
    denom = jnp.sum(p, axis=-1, keepdims=True)
    if sink is not None:
        denom = denom + jnp.exp(sink - m)
    o = jnp.dot(p.astype(BF16), v, preferred_element_type=F32)
    return o / denom


def _head(x, h):
    return x[:, h * HEAD_DIM:(h + 1) * HEAD_DIM]


def _rope(x, cos, sin_signed):
    width = x.shape[1]
    lane = lax.broadcasted_iota(jnp.int32, x.shape, 1)
    first = (lane % 32) < 16
    partner = jnp.where(first, pltpu.roll(x, width - 16, 1), pltpu.roll(x, 16, 1))
    return x * cos + partner * sin_signed


def _rope_tables(n_heads):
    half = HEAD_DIM // 2
    inv = ROPE_THETA ** (-np.arange(0, half, 2, dtype=np.float64) / half)
    t = np.arange(DEC_SEQ)
    inv = inv.astype(np.float32).astype(np.float64)
    ang_r = (t // GRID_W)[:, None].astype(np.float64) * inv[None, :]
    ang_c = (t % GRID_W)[:, None].astype(np.float64) * inv[None, :]
    cos = np.concatenate([np.cos(ang_r), np.cos(ang_r), np.cos(ang_c), np.cos(ang_c)], -1)
    sin = np.concatenate([-np.sin(ang_r), np.sin(ang_r), -np.sin(ang_c), np.sin(ang_c)], -1)
    cos = np.tile(cos, (1, n_heads)).astype(np.float32)
    sin = np.tile(sin, (1, n_heads)).astype(np.float32)
    return cos, sin


def _ctx_attn_kernel(layer, sink_ref, qa_ref, ka_ref, va_ref, qc_ref, kvc_ref, oa_ref, oc_ref):
    qa = (qa_ref[...] * ATTN_SCALE).astype(BF16)
    ka = ka_ref[...].astype(BF16)
    va = va_ref[...].astype(BF16)
    outs = [_attend(_head(qa, h), _head(ka, h), _head(va, h)) for h in range(NA_HEADS)]
    oa_ref[...] = jnp.concatenate(outs, axis=-1).astype(BF16)

    qc = (qc_ref[...] * ATTN_SCALE).astype(BF16)
    kvc = kvc_ref[...].astype(BF16)
    kc, vc = kvc[:, :SWA_KV_W], kvc[:, SWA_KV_W:]
    outs = []
    for h in range(SWA_HEADS):
        hk = h // SWA_GROUP
        outs.append(_attend(_head(qc, h), _head(kc, hk), _head(vc, hk), sink=sink_ref[layer, h]))
    oc_ref[...] = jnp.concatenate(outs, axis=-1).astype(BF16)


def _ctx_attention(proj, swa_sink, layer):
    blk = lambda w, c: pl.BlockSpec((SEQ, w), lambda b: (b, c // w))
    return pl.pallas_call(
        functools.partial(_ctx_attn_kernel, layer),
        grid=(BATCH,),
        in_specs=[
            pl.BlockSpec(memory_space=pltpu.SMEM),
            blk(NA_W, COL_QA), blk(NA_W, COL_KA), blk(NA_W, COL_VA),
            blk(SWA_W, COL_QC), blk(2 * SWA_KV_W, COL_KC),
        ],
        out_specs=[pl.BlockSpec((SEQ, NA_W), lambda b: (b, 0)),
                   pl.BlockSpec((SEQ, SWA_W), lambda b: (b, 0))],
        out_shape=[jax.ShapeDtypeStruct((NP, NA_W), BF16),
                   jax.ShapeDtypeStruct((NP, SWA_W), BF16)],
        compiler_params=_params(1),
        name="ctx_attn",
    )(swa_sink, proj, proj, proj, proj, proj)


_NA_BAND = NA_WIN_ROWS * GRID_W


def _na_band_start(r):
    return jnp.clip(r - NA_WIN_ROWS // 2, 0, GRID_ROWS - NA_WIN_ROWS)


def _na_kernel(q_ref, k_ref, v_ref, ck_ref, cv_ref, bias_ref, o_ref):
    r = pl.program_id(1)
    start = pl.multiple_of(_na_band_start(r) * GRID_W, GRID_W)
    q = (q_ref[...] * ATTN_SCALE).astype(BF16)
    k = jnp.concatenate([k_ref[pl.ds(start, _NA_BAND), :].astype(BF16), ck_ref[...].astype(BF16)], axis=0)
    v = jnp.concatenate([v_ref[pl.ds(start, _NA_BAND), :].astype(BF16), cv_ref[...].astype(BF16)], axis=0)
    heads = lambda x: jnp.stack([_head(x, h) for h in range(NA_HEADS)], axis=0)
    s = jnp.einsum('hqd,hkd->hqk', heads(q), heads(k), preferred_element_type=F32) + bias_ref[...]
    m = jnp.max(s, axis=-1, keepdims=True)
    p = jnp.exp(s - m)
    denom = jnp.sum(p, axis=-1, keepdims=True)
    o = jnp.einsum('hqk,hkd->hqd', p.astype(BF16), heads(v), preferred_element_type=F32) / denom
    o_ref[...] = jnp.concatenate([o[h] for h in range(NA_HEADS)], axis=-1).astype(BF16)


def _na_bias_tables(rpb_l):
    nc = 2 * NA_WIN_COLS - 1
    col = np.arange(GRID_W)
    dc = np.clip(col[None, :] - col[:, None], -(NA_WIN_COLS - 1), NA_WIN_COLS - 1) + NA_WIN_COLS - 1
    onehot = (dc.reshape(-1)[None, :] == np.arange(nc)[:, None]).astype(np.float32)
    cs = np.clip(col - NA_WIN_COLS // 2, 0, GRID_W - NA_WIN_COLS)
    in_win = (col[None, :] >= cs[:, None]) & (col[None, :] < cs[:, None] + NA_WIN_COLS)
    toe = jnp.einsum('hrd,dn->hrn', rpb_l.astype(F32), jnp.asarray(onehot),
                     precision=lax.Precision.HIGHEST)
    toe = toe.reshape(NA_HEADS, 2 * NA_WIN_ROWS - 1, GRID_W, GRID_W)
    toe = jnp.where(jnp.asarray(in_win)[None, None], toe, NEG_INF)
    strips = jnp.stack([toe[:, o:o + NA_WIN_ROWS] for o in range(NA_WIN_ROWS)], 0)
    strips = strips.transpose(0, 1, 3, 2, 4).reshape(NA_WIN_ROWS, NA_HEADS, GRID_W, _NA_BAND)
    return jnp.pad(strips, ((0, 0), (0, 0), (0, 0), (0, PAST_LEN)))


def _na_latent(proj, cache_k, cache_v, rpb_l, layer):
    bias = _na_bias_tables(rpb_l)
    row0 = NP // GRID_W
    qmap = lambda b, r: (row0 + b * GRID_ROWS + r, 0)
    kvspec = lambda c: pl.BlockSpec((DEC_SEQ, NA_W), lambda b, r: (NP // DEC_SEQ + b, c // NA_W))
    cspec = pl.BlockSpec((None, None, PAST_LEN, NA_W), lambda b, r: (b, layer, 0, 0))

    def bias_map(b, r):
        return (_na_band_start(r) - r + NA_WIN_ROWS - 1, 0, 0, 0)

    return pl.pallas_call(
        _na_kernel,
        grid=(DEC_BATCH, GRID_ROWS),
        in_specs=[
            pl.BlockSpec((GRID_W, NA_W), qmap),
            kvspec(COL_KA), kvspec(COL_VA), cspec, cspec,
            pl.BlockSpec((None, NA_HEADS, GRID_W, _NA_BAND + PAST_LEN), bias_map),
        ],
        out_specs=pl.BlockSpec((GRID_W, NA_W), lambda b, r: (b * GRID_ROWS + r, 0)),
        out_shape=jax.ShapeDtypeStruct((NS, NA_W), BF16),
        compiler_params=_params(2),
        name="na_latent",
    )(proj, proj, proj, cache_k.reshape(DEC_BATCH, DEPTH, PAST_LEN, NA_W),
      cache_v.reshape(DEC_BATCH, DEPTH, PAST_LEN, NA_W), bias)


_SWA_KEYS = 3 * SWA_BLOCK


def _swa_kernel(layer, sink_ref, q_ref, kv_ref, ck_ref, cv_ref, cq_ref, sq_ref, ckt_ref, skt_ref,
                o_ref):
    n = pl.program_id(1)
    start = pl.multiple_of(jnp.clip((n - 1) * SWA_BLOCK, 0, DEC_SEQ - _SWA_KEYS), SWA_BLOCK)
    q = (_rope(q_ref[...], cq_ref[...], sq_ref[...]) * ATTN_SCALE).astype(BF16)
    kv = kv_ref[pl.ds(start, _SWA_KEYS), :]
    kw = _rope(kv[:, :SWA_KV_W], ckt_ref[pl.ds(start, _SWA_KEYS), :], skt_ref[pl.ds(start, _SWA_KEYS), :])
    k = jnp.concatenate([kw.astype(BF16), ck_ref[...].astype(BF16)], axis=0)
    v = jnp.concatenate([kv[:, SWA_KV_W:].astype(BF16), cv_ref[...].astype(BF16)], axis=0)
    nk = _SWA_KEYS + PAST_LEN
    qpos = n * SWA_BLOCK + lax.broadcasted_iota(jnp.int32, (SWA_BLOCK, nk), 0)
    kidx = lax.broadcasted_iota(jnp.int32, (SWA_BLOCK, nk), 1)
    valid = (jnp.abs(qpos - (start + kidx)) <= SWA_WINDOW) | (kidx >= _SWA_KEYS)
    outs = []
    for h in range(SWA_HEADS):
        hk = h // SWA_GROUP
        outs.append(_attend(_head(q, h), _head(k, hk), _head(v, hk), valid=valid, sink=sink_ref[layer, h]))
    o_ref[...] = jnp.concatenate(outs, axis=-1).astype(BF16)


def _swa_latent(proj, cache_k, cache_v, swa_sink, layer):
    cq, sq = _rope_tables(SWA_HEADS)
    ck, sk = _rope_tables(SWA_KV_HEADS)
    row0 = NP // SWA_BLOCK
    nblk = DEC_SEQ // SWA_BLOCK
    qmap = lambda b, n: (row0 + b * nblk + n, COL_QC // SWA_W)
    omap = lambda b, n: (b * nblk + n, 0)
    cspec = pl.BlockSpec((None, None, PAST_LEN, SWA_KV_W), lambda b, n: (b, layer, 0, 0))
    tq = pl.BlockSpec((SWA_BLOCK, SWA_W), lambda b, n: (n, 0))
    tk = pl.BlockSpec((DEC_SEQ, SWA_KV_W), lambda b, n: (0, 0))
    return pl.pallas_call(
        functools.partial(_swa_kernel, layer),
        grid=(DEC_BATCH, nblk),
        in_specs=[
            pl.BlockSpec(memory_space=pltpu.SMEM),
            pl.BlockSpec((SWA_BLOCK, SWA_W), qmap),
            pl.BlockSpec((DEC_SEQ, 2 * SWA_KV_W), lambda b, n: (NP // DEC_SEQ + b, COL_KC // (2 * SWA_KV_W))),
            cspec, cspec, tq, tq, tk, tk,
        ],
        out_specs=pl.BlockSpec((SWA_BLOCK, SWA_W), omap),
        out_shape=jax.ShapeDtypeStruct((NS, SWA_W), BF16),
        compiler_params=_params(2),
        name="swa_latent",
    )(swa_sink, proj, proj, cache_k.reshape(DEC_BATCH, DEPTH, PAST_LEN, SWA_KV_W),
      cache_v.reshape(DEC_BATCH, DEPTH, PAST_LEN, SWA_KV_W),
      jnp.asarray(cq), jnp.asarray(sq), jnp.asarray(ck), jnp.asarray(sk))


def _dft_mats():
    bk, n = HY_BLK, HY_NFFT
    s = np.arange(bk)
    ang = 2.0 * np.pi * ((s[:, None] * s[None, :]) % n) / n
    fwd = np.zeros((n, bk))
    fwd[:bk] = np.cos(ang)
    fwd[bk:] = -np.sin(ang)
    fwd[bk] = (-1.0) ** s
    inv = np.zeros((bk, n))
    inv[:, :bk] = (2.0 / n) * np.cos(ang)
    inv[:, 0] = 1.0 / n
    inv[:, bk:] = -(2.0 / n) * np.sin(ang)
    inv[:, bk] = (1.0 / n) * (-1.0) ** s
    fh, fl = _bf16_split_np(fwd)
    ih, il = _bf16_split_np(inv)
    return np.concatenate([fh, fh, fl], axis=1), np.concatenate([ih, ih, il], axis=1)


def _split3(x):
    hi = x.astype(BF16)
    lo = (x - hi.astype(F32)).astype(BF16)
    return jnp.concatenate([hi, lo, hi], axis=0)


def _hy_features(L):
    pos = np.arange(L, dtype=np.float64)
    t = (pos.astype(np.float32) / np.float32(max(L - 1, 1))).astype(np.float64)
    bands = np.linspace(1e-4, HY_POS_BANDS - 1, HY_POS_BANDS, dtype=np.float32).astype(np.float64)
    ang = np.float64(np.float32(2.0 * math.pi / L)) * pos[:, None] * bands[None, :]
    feat = np.concatenate([t[:, None], np.cos(ang), -np.sin(ang)], -1)
    feat = np.pad(feat, ((0, 0), (0, HY_FFN - HY_POS_DIM)))
    deltas = np.abs(np.linspace(math.log(HY_DECAY_TARGET) / HY_SLOW_DECAY,
                                math.log(HY_DECAY_TARGET) / HY_FAST_DECAY, HY_WIDTH, dtype=np.float32))
    window = np.exp(-t[:, None] * deltas[None, :].astype(np.float64)) + HY_MOD_SHIFT
    return feat.astype(np.float32), window.astype(np.float32)


_HYF_TC = 512


def _hy_filter_kernel(feat_ref, w1_ref, b1_ref, f1_ref, w2_ref, b2_ref, f2_ref, w3f_ref, w3b_ref, win_ref,
                      hf_ref, hb_ref):
    hp = lax.Precision.HIGHEST
    z = jnp.dot(feat_ref[...], w1_ref[...], precision=hp, preferred_element_type=F32) + b1_ref[...]
    h = jnp.sin(f1_ref[...] * z)
    z = jnp.dot(h, w2_ref[...], precision=hp, preferred_element_type=F32) + b2_ref[...]
    h = jnp.sin(f2_ref[...] * z)
    win = win_ref[...]
    af = jnp.dot(h, w3f_ref[...], precision=hp, preferred_element_type=F32) * win
    ab = jnp.dot(h, w3b_ref[...], precision=hp, preferred_element_type=F32) * win
    nrm = jnp.sum(jnp.abs(af), axis=0, keepdims=True) + jnp.sum(jnp.abs(ab), axis=0, keepdims=True)
    hf_ref[...] = af / nrm
    hb_ref[...] = ab / nrm


def _hy_filter(L, layer, hy_w1p, hy_b1, hy_freq1, hy_w2, hy_b2, hy_freq2, hy_w3):
    feat, window = _hy_features(L)
    tc = _HYF_TC
    ncc = HY_WIDTH // tc
    ow = HY_ORDER * HY_WIDTH
    small = lambda shape: pl.BlockSpec((None,) + shape, lambda o, c: (layer,) + (0,) * len(shape))
    w3spec = lambda d: pl.BlockSpec((None, HY_FFN, tc), lambda o, c: (layer, 0, (d * HY_ORDER + o) * ncc + c))
    ospec = pl.BlockSpec((L, tc), lambda o, c: (0, o * ncc + c))
    vec = lambda a: a.reshape(DEPTH, 1, HY_FFN)
    return pl.pallas_call(
        _hy_filter_kernel,
        grid=(HY_ORDER, ncc),
        in_specs=[
            pl.BlockSpec((L, HY_FFN), lambda o, c: (0, 0)),
            small((HY_FFN, HY_FFN)), small((1, HY_FFN)), small((1, HY_FFN)),
            small((HY_FFN, HY_FFN)), small((1, HY_FFN)), small((1, HY_FFN)),
            w3spec(0), w3spec(1),
            pl.BlockSpec((L, tc), lambda o, c: (0, c)),
        ],
        out_specs=[ospec, ospec],
        out_shape=[jax.ShapeDtypeStruct((L, ow), F32), jax.ShapeDtypeStruct((L, ow), F32)],
        compiler_params=_params(2),
        name=f"hy_filter_{L}",
    )(jnp.asarray(feat), hy_w1p, vec(hy_b1), vec(hy_freq1), hy_w2, vec(hy_b2), vec(hy_freq2),
      hy_w3, hy_w3, jnp.asarray(window))


_HYS_TC = 256


def _hy_spectrum_kernel(nb, hf_ref, hb_ref, f3_ref, g_ref):
    n, bk = HY_NFFT, HY_BLK
    row = lax.broadcasted_iota(jnp.int32, (n, 1), 0)
    ones_lo = (row <= bk).astype(F32)
    sgn = jnp.where(row % 2 == 0, 1.0, -1.0).astype(F32)
    conj = jnp.where(row > bk, -1.0, 1.0).astype(F32)
    f3 = f3_ref[...]

    def spectra(h_ref, drop_lag0):
        first, tail = [], []
        for b in range(nb):
            blk = h_ref[b * bk:(b + 1) * bk, :]
            t = jnp.dot(f3, _split3(blk), preferred_element_type=F32)
            head = ones_lo * blk[0:1, :]
            if b == 0 and drop_lag0:
                t = t - head
                tail.append(sgn * t)
            else:
                tail.append(sgn * (t - head))
            first.append(t)
        return [first[d] + (tail[d - 1] if d >= 1 else 0.0) for d in range(nb)]

    gf = spectra(hf_ref, False)
    gb = spectra(hb_ref, True)
    g_ref[nb - 1] = gf[0] + conj * gb[0]
    for d in range(1, nb):
        g_ref[nb - 1 + d] = gf[d]
        g_ref[nb - 1 - d] = conj * gb[d]


def _hy_spectrum(hf, hb, f3):
    L = hf.shape[0]
    nb = L // HY_BLK
    ow = HY_ORDER * HY_WIDTH
    tc = _HYS_TC
    return pl.pallas_call(
        functools.partial(_hy_spectrum_kernel, nb),
        grid=(ow // tc,),
        in_specs=[
            pl.BlockSpec((L, tc), lambda c: (0, c)),
            pl.BlockSpec((L, tc), lambda c: (0, c)),
            pl.BlockSpec((HY_NFFT, 3 * HY_BLK), lambda c: (0, 0)),
        ],
        out_specs=pl.BlockSpec((2 * nb - 1, HY_NFFT, tc), lambda c: (0, 0, c)),
        out_shape=jax.ShapeDtypeStruct((2 * nb - 1, HY_NFFT, ow), F32),
        compiler_params=_params(1),
        name=f"hy_spectrum_{L}",
    )(hf, hb, f3)


def _spec_mul(x, g):
    bk = HY_BLK
    xa, xb = x[:bk], x[bk:]
    ga, gb = g[:bk], g[bk:]
    row0 = lax.broadcasted_iota(jnp.int32, xa.shape, 0) == 0
    bb = xb * gb
    pa = xa * ga - jnp.where(row0, 0.0, bb)
    pb = jnp.where(row0, bb, xa * gb + xb * ga)
    return jnp.concatenate([pa, pb], axis=0)


def _short_conv(u, w, b):
    L = u.shape[0]
    row = lax.broadcasted_iota(jnp.int32, u.shape, 0)
    prev = jnp.where(row == 0, 0.0, pltpu.roll(u, 1, 0))
    nxt = jnp.where(row == L - 1, 0.0, pltpu.roll(u, L - 1, 0))
    return prev * w[0:1] + u * w[1:2] + nxt * w[2:3] + b


def _hy_conv_kernel(nb, uv_ref, u1_ref, u2_ref, wv_ref, w1_ref, w2_ref, bv_ref, b1_ref, b2_ref, skip_ref,
                    g0_ref, g1_ref, f3_ref, i3_ref, o_ref):
    bk = HY_BLK
    v = _short_conv(uv_ref[...], wv_ref[...], bv_ref[...])
    gates = (_short_conv(u1_ref[...], w1_ref[...], b1_ref[...]),
             _short_conv(u2_ref[...], w2_ref[...], b2_ref[...]))
    cc = v.shape[1]
    f3 = f3_ref[...]
    i3 = i3_ref[...]
    z = v
    for o, g_ref in enumerate((g0_ref, g1_ref)):
        zcat = jnp.concatenate([z[j * bk:(j + 1) * bk] for j in range(nb)], axis=1)
        x = jnp.dot(f3, _split3(zcat), preferred_element_type=F32)
        prods = []
        for i in range(nb):
            acc = None
            for j in range(nb):
                t = _spec_mul(x[:, j * cc:(j + 1) * cc], g_ref[i - j + nb - 1])
                acc = t if acc is None else acc + t
            prods.append(acc)
        pcat = jnp.concatenate(prods, axis=1)
        y = jnp.dot(i3, _split3(pcat), preferred_element_type=F32)
        y = jnp.concatenate([y[:, i * cc:(i + 1) * cc] for i in range(nb)], axis=0)
        z = gates[o] * (y + skip_ref[o:o + 1, :] * z)
    o_ref[...] = z.astype(BF16)


def _hy_conv(proj, g, f3, i3, short_w, short_b, skip, layer, L, row0, nbatch, cc):
    nb = L // HY_BLK
    ncc = HY_WIDTH // cc
    rb0 = row0 // L
    ucol = lambda part: pl.BlockSpec((L, cc), lambda c, b: (rb0 + b, (COL_HY + part * HY_WIDTH) // cc + c))
    wcol = lambda part: pl.BlockSpec((None, HY_SHORT, cc), lambda c, b: (layer, 0, part * ncc + c))
    bcol = lambda part: pl.BlockSpec((None, 1, cc), lambda c, b: (layer, 0, part * ncc + c))
    gspec = lambda o: pl.BlockSpec((2 * nb - 1, HY_NFFT, cc), lambda c, b: (0, 0, o * ncc + c))
    in_specs = [
        ucol(0), ucol(1), ucol(2), wcol(0), wcol(1), wcol(2), bcol(0), bcol(1), bcol(2),
        pl.BlockSpec((None, HY_ORDER, cc), lambda c, b: (layer, 0, c)),
        gspec(0), gspec(1),
        pl.BlockSpec((HY_NFFT, 3 * HY_BLK), lambda c, b: (0, 0)),
        pl.BlockSpec((HY_BLK, 3 * HY_NFFT), lambda c, b: (0, 0)),
    ]
    args = [proj, proj, proj, short_w, short_w, short_w] + [short_b.reshape(DEPTH, 1, 3 * HY_WIDTH)] * 3 + [
        skip, g, g, f3, i3]
    return pl.pallas_call(
        functools.partial(_hy_conv_kernel, nb),
        grid=(ncc, nbatch),
        in_specs=in_specs,
        out_specs=pl.BlockSpec((L, cc), lambda c, b: (b, c)),
        out_shape=jax.ShapeDtypeStruct((nbatch * L, HY_WIDTH), BF16),
        compiler_params=_params(2),
        name=f"hy_conv_{L}",
    )(*args)


_OUT_TM = 512
_OUT_TK = 512


def _layer_norm(x, g, b):
    mu = jnp.mean(x, axis=-1, keepdims=True)
    xc = x - mu
    var = jnp.mean(xc * xc, axis=-1, keepdims=True)
    return xc * lax.rsqrt(var + LN_EPS) * g + b


def _outproj_kernel(oa_c, ob_c, oc_c, oa_l, ob_l, oc_l, w_ref, x_ref, g1_ref, lg_ref, lb_ref, sc_ref, sh_ref,
                    x1_ref, h_ref, acc_ref):
    i, k = pl.program_id(0), pl.program_id(1)
    w = w_ref[...].astype(BF16)
    is_ctx = i < NP // _OUT_TM

    def mixer(c_ref, l_ref):
        return jnp.where(is_ctx, c_ref[...], l_ref[...])

    @pl.when(k == 0)
    def _():
        acc_ref[...] = jnp.dot(mixer(oa_c, oa_l), w, preferred_element_type=F32)

    @pl.when((k == 1) | (k == 2))
    def _():
        acc_ref[...] += jnp.dot(mixer(ob_c, ob_l), w, preferred_element_type=F32)

    @pl.when(k == 3)
    def _():
        mix = acc_ref[...] + jnp.dot(mixer(oc_c, oc_l), w, preferred_element_type=F32)
        x1 = _layer_norm(DEEPNORM_ALPHA * x_ref[...] + g1_ref[...] * mix, lg_ref[...], lb_ref[...])
        x1_ref[...] = x1
        h_ref[...] = (x1 * (1.0 + sc_ref[...]) + sh_ref[...]).astype(h_ref.dtype)


def _outproj(ctx_mix, lat_mix, w_out, x, mod, ln_g, ln_b, layer, h_dtype):
    tm, tk = _OUT_TM, _OUT_TK
    npt = NP // tm
    vec = pl.BlockSpec((None, 1, D_MODEL), lambda i, k: (layer, 0, 0))
    crow = lambda i: jnp.minimum(i, npt - 1)
    lrow = lambda i: jnp.maximum(i - npt, 0)
    kb = lambda k: jnp.clip(k - 1, 0, 1)
    return pl.pallas_call(
        _outproj_kernel,
        grid=(NT // tm, MIX_W // tk),
        in_specs=[
            pl.BlockSpec((tm, tk), lambda i, k: (crow(i), 0)),
            pl.BlockSpec((tm, tk), lambda i, k: (crow(i), kb(k))),
            pl.BlockSpec((tm, tk), lambda i, k: (crow(i), 0)),
            pl.BlockSpec((tm, tk), lambda i, k: (lrow(i), 0)),
            pl.BlockSpec((tm, tk), lambda i, k: (lrow(i), kb(k))),
            pl.BlockSpec((tm, tk), lambda i, k: (lrow(i), 0)),
            pl.BlockSpec((None, tk, D_MODEL), lambda i, k: (layer, k, 0)),
            pl.BlockSpec((tm, D_MODEL), lambda i, k: (i, 0)),
            _mod_spec(layer, 2, tm), vec, vec, _mod_spec(layer, 4, tm), _mod_spec(layer, 3, tm),
        ],
        out_specs=[pl.BlockSpec((tm, D_MODEL), lambda i, k: (i, 0)),
                   pl.BlockSpec((tm, D_MODEL), lambda i, k: (i, 0))],
        out_shape=[jax.ShapeDtypeStruct((NT, D_MODEL), F32), jax.ShapeDtypeStruct((NT, D_MODEL), h_dtype)],
        scratch_shapes=[pltpu.VMEM((tm, D_MODEL), F32)],
        compiler_params=_params(2),
        name="outproj_ln",
    )(*ctx_mix, *lat_mix, w_out, x, mod, ln_g.reshape(DEPTH, 1, D_MODEL), ln_b.reshape(DEPTH, 1, D_MODEL),
      mod, mod)


_FFN_TM = 1024
_FFN_TF = 256


def _swiglu_partial(h, wg_ref, wu_ref, wd_ref):
    g = jnp.dot(h, wg_ref[...].astype(BF16), preferred_element_type=F32)
    u = jnp.dot(h, wu_ref[...].astype(BF16), preferred_element_type=F32)
    a = (g * jax.nn.sigmoid(g) * u).astype(BF16)
    return jnp.dot(a, wd_ref[...].astype(BF16), preferred_element_type=F32)


def _ffn_kernel(h_ref, wg_ref, wu_ref, wd_ref, y_ref):
    @pl.when(pl.program_id(1) == 0)
    def _():
        y_ref[...] = jnp.zeros_like(y_ref)

    y_ref[...] += _swiglu_partial(h_ref[...], wg_ref, wu_ref, wd_ref)


def _ffn_dense(h, wg, wu, wd, idx):
    tm, tf = _FFN_TM, _FFN_TF
    return pl.pallas_call(
        _ffn_kernel,
        grid=(NT // tm, D_FF // tf),
        in_specs=[
            pl.BlockSpec((tm, D_MODEL), lambda i, f: (i, 0)),
            pl.BlockSpec((None, D_MODEL, tf), lambda i, f: (idx, 0, f)),
            pl.BlockSpec((None, D_MODEL, tf), lambda i, f: (idx, 0, f)),
            pl.BlockSpec((None, tf, D_MODEL), lambda i, f: (idx, f, 0)),
        ],
        out_specs=pl.BlockSpec((tm, D_MODEL), lambda i, f: (i, 0)),
        out_shape=jax.ShapeDtypeStruct((NT, D_MODEL), F32),
        compiler_params=_params(2),
        name="ffn_dense",
    )(h, wg, wu, wd)


_RT_TM = 1024


def _router_kernel(h_ref, rt_ref, r_ref):
    logits = lax.dot_general(rt_ref[...].astype(BF16), h_ref[...].astype(BF16), (((1,), (1,)), ((), ())),
                             preferred_element_type=F32)
    idx = lax.broadcasted_iota(jnp.int32, logits.shape, 0)
    m1 = jnp.max(logits, axis=0, keepdims=True)
    i1 = jnp.min(jnp.where(logits == m1, idx, N_EXPERTS), axis=0, keepdims=True)
    rest = jnp.where(idx == i1, -jnp.inf, logits)
    m2 = jnp.max(rest, axis=0, keepdims=True)
    i2 = jnp.min(jnp.where(rest == m2, idx, N_EXPERTS), axis=0, keepdims=True)
    e2 = jnp.exp(m2 - m1)
    g1 = 1.0 / (1.0 + e2)
    g2 = e2 / (1.0 + e2)
    r_ref[...] = (jnp.where(idx == 0, i1.astype(F32), 0.0) + jnp.where(idx == 1, i2.astype(F32), 0.0)
                  + jnp.where(idx == 2, g1, 0.0) + jnp.where(idx == 3, g2, 0.0))


def _router(h, router_t):
    tm = _RT_TM
    return pl.pallas_call(
        _router_kernel,
        grid=(NT // tm,),
        in_specs=[pl.BlockSpec((tm, D_MODEL), lambda i: (i, 0)),
                  pl.BlockSpec((N_EXPERTS, D_MODEL), lambda i: (0, 0))],
        out_specs=pl.BlockSpec((N_EXPERTS, tm), lambda i: (0, i)),
        out_shape=jax.ShapeDtypeStruct((N_EXPERTS, NT), F32),
        compiler_params=_params(1),
        name="moe_router",
    )(h, router_t)


_MOE_TM = 1152
_MOE_CHUNK = 384
_MOE_TF = 256
_MOE_TILES = (TOP_K * NT + N_EXPERTS * (_MOE_TM - 1)) // _MOE_TM
_MOE_ROWS = _MOE_TILES * _MOE_TM


def _moe_routed_kernel(te_ref, nv_ref, src_ref, dst_ref, h_hbm, gate_ref, wg_ref, wu_ref, wd_ref,
                       y_hbm, xs, xb, acc, gsem, ssem):
    del te_ref
    i, f = pl.program_id(0), pl.program_id(1)
    tm = _MOE_TM
    nv = nv_ref[i]

    def gather_copy(tile, slot, r):
        return pltpu.make_async_copy(h_hbm.at[pl.ds(src_ref[tile * tm + r], 1), :],
                                     xs.at[slot, pl.ds(r, 1), :], gsem.at[slot])

    def scatter_copy(tile, r):
        return pltpu.make_async_copy(acc.at[pl.ds(r, 1), :],
                                     y_hbm.at[pl.ds(dst_ref[tile * tm + r], 1), :], ssem.at[0])

    def for_rows(tile, fn):
        def body(r, carry):
            fn(r)
            return carry
        lax.fori_loop(0, nv_ref[tile], body, 0)

    @pl.when(f == 0)
    def _():
        @pl.when(i == 0)
        def _():
            xs[...] = jnp.zeros_like(xs)
            for_rows(0, lambda r: gather_copy(0, 0, r).start())

        @pl.when(i > 0)
        def _():
            for_rows(i - 1, lambda r: scatter_copy(i - 1, r).wait())

        slot = i % 2
        for_rows(i, lambda r: gather_copy(i, slot, r).wait())

        @pl.when(i + 1 < _MOE_TILES)
        def _():
            for_rows(i + 1, lambda r: gather_copy(i + 1, 1 - slot, r).start())

        xb[...] = xs[slot].astype(BF16)
        acc[...] = jnp.zeros_like(acc)

    wg = wg_ref[...].astype(BF16)
    wu = wu_ref[...].astype(BF16)
    wd = wd_ref[...].astype(BF16)
    for c in range(tm // _MOE_CHUNK):
        rows = slice(c * _MOE_CHUNK, (c + 1) * _MOE_CHUNK)

        @pl.when(c * _MOE_CHUNK < nv)
        def _():
            h = xb[rows, :]
            g = jnp.dot(h, wg, preferred_element_type=F32)
            u = jnp.dot(h, wu, preferred_element_type=F32)
            a = (g * jax.nn.sigmoid(g) * u).astype(BF16)
            acc[rows, :] += jnp.dot(a, wd, preferred_element_type=F32)

    @pl.when(f == pl.num_programs(1) - 1)
    def _():
        acc[...] = acc[...] * gate_ref[...]
        for_rows(i, lambda r: scatter_copy(i, r).start())

        @pl.when(i == _MOE_TILES - 1)
        def _():
            for_rows(i, lambda r: scatter_copy(i, r).wait())


def _moe(h, router, wg, wu, wd, idx):
    tm, tf = _MOE_TM, _MOE_TF
    nf = D_FF_EXPERT // tf
    r = _router(h, router[idx].T)
    pairs = TOP_K * NT
    e_flat = r[0:TOP_K].astype(jnp.int32).reshape(pairs)
    gate_flat = r[TOP_K:2 * TOP_K].reshape(pairs)
    onehot = (e_flat[:, None] == jnp.arange(N_EXPERTS, dtype=jnp.int32)[None, :]).astype(jnp.int32)
    csum = jnp.cumsum(onehot, axis=0)
    rank = jnp.take_along_axis(csum, e_flat[:, None], axis=1)[:, 0] - 1
    counts = csum[-1]
    tiles_e = (counts + tm - 1) // tm
    tile_end = jnp.cumsum(tiles_e)
    tile_start = tile_end - tiles_e
    pos = tile_start[e_flat] * tm + rank
    pair_id = jnp.arange(pairs, dtype=jnp.int32)
    dst = jnp.zeros((_MOE_ROWS,), jnp.int32).at[pos].set(pair_id, unique_indices=True)
    src = dst % NT
    gates = gate_flat[dst].reshape(_MOE_ROWS, 1)
    tile = jnp.arange(_MOE_TILES, dtype=jnp.int32)
    te = jnp.minimum(jnp.sum((tile[:, None] >= tile_end[None, :]).astype(jnp.int32), axis=1), N_EXPERTS - 1)
    nv = jnp.clip(counts[te] - (tile - tile_start[te]) * tm, 0, tm).astype(jnp.int32)

    def wmap(i, f, te_ref, nv_ref, *_):
        return (idx, te_ref[i], 0, jnp.where(nv_ref[i] > 0, f, nf - 1))

    def wdmap(i, f, te_ref, nv_ref, *_):
        return (idx, te_ref[i], jnp.where(nv_ref[i] > 0, f, nf - 1), 0)

    grid_spec = pltpu.PrefetchScalarGridSpec(
        num_scalar_prefetch=4,
        grid=(_MOE_TILES, nf),
        in_specs=[
            pl.BlockSpec(memory_space=pl.ANY),
            pl.BlockSpec((tm, 1), lambda i, f, *_: (i, 0)),
            pl.BlockSpec((None, None, D_MODEL, tf), wmap),
            pl.BlockSpec((None, None, D_MODEL, tf), wmap),
            pl.BlockSpec((None, None, tf, D_MODEL), wdmap),
        ],
        out_specs=pl.BlockSpec(memory_space=pl.ANY),
        scratch_shapes=[
            pltpu.VMEM((2, tm, D_MODEL), F32),
            pltpu.VMEM((tm, D_MODEL), BF16),
            pltpu.VMEM((tm, D_MODEL), F32),
            pltpu.SemaphoreType.DMA((2,)),
            pltpu.SemaphoreType.DMA((1,)),
        ],
    )
    return pl.pallas_call(
        _moe_routed_kernel,
        grid_spec=grid_spec,
        out_shape=jax.ShapeDtypeStruct((pairs, D_MODEL), F32),
        compiler_params=_params(2),
        name="moe_routed",
    )(te, nv, src, dst, h, gates, wg, wu, wd)


_EPI_TM = 512


def _epilogue_kernel(n_parts, x_ref, *refs):
    y_refs = refs[:n_parts]
    g2_ref, lg_ref, lb_ref, sc_ref, sh_ref, x2_ref, xm_ref = refs[n_parts:]
    y = y_refs[0][...]
    for y_ref in y_refs[1:]:
        y = y + y_ref[...]
    x2 = _layer_norm(DEEPNORM_ALPHA * x_ref[...] + g2_ref[...] * y, lg_ref[...], lb_ref[...])
    x2_ref[...] = x2
    xm_ref[...] = (x2 * (1.0 + sc_ref[...]) + sh_ref[...]).astype(BF16)


def _epilogue(x1, y, mod, ln_g, ln_b, layer):
    tm = _EPI_TM
    n_parts = y.shape[0] // NT
    nxt = min(layer + 1, DEPTH - 1)
    vec = pl.BlockSpec((None, 1, D_MODEL), lambda i: (layer, 0, 0))
    row = pl.BlockSpec((tm, D_MODEL), lambda i: (i, 0))
    parts = [pl.BlockSpec((tm, D_MODEL), functools.partial(lambda p, i: (p * (NT // tm) + i, 0), p))
             for p in range(n_parts)]
    return pl.pallas_call(
        functools.partial(_epilogue_kernel, n_parts),
        grid=(NT // tm,),
        in_specs=[row] + parts + [_mod_spec(layer, 5, tm), vec, vec, _mod_spec(nxt, 1, tm),
                                  _mod_spec(nxt, 0, tm)],
        out_specs=[row, row],
        out_shape=[jax.ShapeDtypeStruct((NT, D_MODEL), F32), jax.ShapeDtypeStruct((NT, D_MODEL), BF16)],
        compiler_params=_params(1),
        name="ffn_epilogue",
    )(x1, *([y] * n_parts), mod, ln_g.reshape(DEPTH, 1, D_MODEL), ln_b.reshape(DEPTH, 1, D_MODEL), mod, mod)


def kernel(x_prompt, x_sample, cache_na_k, cache_na_v, cache_swa_k, cache_swa_v, c, c_ctx, w_mod, b_mod, w_in, w_out, na_rpb, swa_sink, hy_short_w, hy_short_b, hy_w1, hy_b1, hy_freq1, hy_w2, hy_b2, hy_freq2, hy_w3, hy_skip, ln1_g, ln1_b, ln2_g, ln2_b, ffn_w_gate, ffn_w_up, ffn_w_down, moe_router, moe_w_gate, moe_w_up, moe_w_down):
    cond = jnp.concatenate([c_ctx[None, :], c], axis=0)
    mod = _modulation(cond, w_mod, b_mod)
    x, xm = _prologue(x_prompt.reshape(NP, D_MODEL), x_sample.reshape(NS, D_MODEL), mod)

    f3_np, i3_np = _dft_mats()
    f3, i3 = jnp.asarray(f3_np), jnp.asarray(i3_np)
    hy_w1p = jnp.pad(hy_w1, ((0, 0), (0, HY_FFN - HY_POS_DIM), (0, 0)))

    kv_out = []
    for l in range(DEPTH):
        proj = _inproj(xm, w_in, l)
        kv_out.append(proj[:NP])

        oa_c, oc_c = _ctx_attention(proj, swa_sink, l)
        oa_l = _na_latent(proj, cache_na_k, cache_na_v, na_rpb[l], l)
        oc_l = _swa_latent(proj, cache_swa_k, cache_swa_v, swa_sink, l)

        filt = (l, hy_w1p, hy_b1, hy_freq1, hy_w2, hy_b2, hy_freq2, hy_w3)
        g_ctx = _hy_spectrum(*_hy_filter(SEQ, *filt), f3)
        g_lat = _hy_spectrum(*_hy_filter(DEC_SEQ, *filt), f3)
        ob_c = _hy_conv(proj, g_ctx, f3, i3, hy_short_w, hy_short_b, hy_skip, l, SEQ, 0, BATCH, 512)
        ob_l = _hy_conv(proj, g_lat, f3, i3, hy_short_w, hy_short_b, hy_skip, l, DEC_SEQ, NP, DEC_BATCH, 128)

        x1, h = _outproj((oa_c, ob_c, oc_c), (oa_l, ob_l, oc_l), w_out, x, mod, ln1_g, ln1_b, l,
                         BF16 if l % 2 == 0 else F32)
        if l % 2 == 0:
            y = _ffn_dense(h, ffn_w_gate, ffn_w_up, ffn_w_down, l // 2)
        else:
            y = _moe(h, moe_router, moe_w_gate, moe_w_up, moe_w_down, l // 2)
        x, xm = _epilogue(x1, y, mod, ln2_g, ln2_b, l)

    def cache(col, heads):
        parts = [p[:, col:col + heads * HEAD_DIM].reshape(BATCH, SEQ, heads, HEAD_DIM) for p in kv_out]
        return jnp.stack(parts, axis=1)

    return (x[:NP].reshape(BATCH, SEQ, D_MODEL), x[NP:].reshape(DEC_BATCH, DEC_SEQ, D_MODEL),
            cache(COL_KA, NA_HEADS), cache(COL_VA, NA_HEADS),
            cache(COL_KC, SWA_KV_HEADS), cache(COL_VC, SWA_KV_HEADS))
```
